```python
import math
import jax, jax.numpy as jnp
from jax import lax
import numpy as np

D_MODEL = 1024
BATCH = 8
SEQ = 4096
DEPTH = 1
DEC_BATCH = 8
DEC_SEQ = 32
PAST_LEN = 2048

CHUNK = 64
Q_BLOCK = 128
EPS = 1e-6

GLA_HEADS = 4
GLA_DK = D_MODEL // 2 // GLA_HEADS
GLA_DV = D_MODEL // GLA_HEADS
GLA_KEY = GLA_HEADS * GLA_DK
GLA_VAL = GLA_HEADS * GLA_DV
GLA_RANK = 16
GLA_TAU = 16.0

DIFF_HEADS = 8
DIFF_HD = D_MODEL // (2 * DIFF_HEADS)
DIFF_VD = 2 * DIFF_HD
DIFF_QK = DIFF_HEADS * 2 * DIFF_HD
DIFF_VAL = DIFF_HEADS * DIFF_VD
ROT_DIM = DIFF_HD // 4
ROPE_THETA = 500000.0

IN_SIZES = (GLA_KEY, GLA_KEY, GLA_VAL, GLA_RANK, GLA_VAL,
            DIFF_QK, DIFF_QK, DIFF_VAL, DIFF_VAL,
            D_MODEL, D_MODEL)
IN_TOTAL = sum(IN_SIZES)
IN_SPLITS = tuple(int(s) for s in np.cumsum(IN_SIZES)[:-1])

kernel_name = 'hybrid_gla_diffattn_stream_step'

F32 = jnp.float32


def rmsnorm(x, g):
    xf = x.astype(F32)
    y = xf * lax.rsqrt(jnp.mean(xf * xf, axis=-1, keepdims=True) + EPS) * g.astype(F32)
    return y.astype(x.dtype)


def rope(x, pos):
    half = ROT_DIM // 2
    inv = ROPE_THETA ** (-jnp.arange(0, ROT_DIM, 2, dtype=F32) / ROT_DIM)
    ang = pos.astype(F32)[:, None] * inv[None, :]
    cos = jnp.cos(ang)[:, None, None, :]
    sin = jnp.sin(ang)[:, None, None, :]
    xr = x[..., :ROT_DIM].astype(F32)
    x1, x2 = xr[..., :half], xr[..., half:]
    rot = jnp.concatenate([x1 * cos - x2 * sin, x2 * cos + x1 * sin], axis=-1)
    return jnp.concatenate([rot.astype(x.dtype), x[..., ROT_DIM:]], axis=-1)


def in_proj(xn, w_in, gla_w_a2, gla_b_a):
    B, T = xn.shape[:2]
    u = xn @ w_in
    gq, gk, gv, gr, gg, dq, dk, dv, dg, ma, mb = jnp.split(u, IN_SPLITS, axis=-1)
    log_a = jax.nn.log_sigmoid((gr @ gla_w_a2 + gla_b_a).astype(F32)) / GLA_TAU
    return (gq.reshape(B, T, GLA_HEADS, GLA_DK), gk.reshape(B, T, GLA_HEADS, GLA_DK),
            gv.reshape(B, T, GLA_HEADS, GLA_DV), log_a.reshape(B, T, GLA_HEADS, GLA_DK), gg,
            dq.reshape(B, T, DIFF_HEADS, 2, DIFF_HD), dk.reshape(B, T, DIFF_HEADS, 2, DIFF_HD),
            dv.reshape(B, T, DIFF_HEADS, DIFF_VD), dg, ma, mb)


def gla_block(state, q, k, v, log_a):
    C = q.shape[1]
    b = jnp.cumsum(log_a, axis=1)
    qf = q.astype(F32) * (GLA_DK ** -0.5)
    kf = k.astype(F32)
    vf = v.astype(F32)
    o_inter = jnp.einsum('bthd,bhde->bthe', qf * jnp.exp(b), state)
    causal = jnp.tril(jnp.ones((C, C), dtype=bool))[None, :, :, None, None]
    diff = b[:, :, None] - b[:, None, :]
    decay = jnp.exp(jnp.where(causal, diff, -jnp.inf))
    A = jnp.einsum('bthd,bshd,btshd->bhts', qf, kf, decay)
    o_intra = jnp.einsum('bhts,bshe->bthe', A, vf)
    b_last = b[:, -1]
    k_dec = kf * jnp.exp(b_last[:, None] - b)
    new_state = jnp.exp(b_last)[..., None] * state + jnp.einsum('bshd,bshe->bhde', k_dec, vf)
    return new_state, o_inter + o_intra


def gla_prompt(q, k, v, log_a):
    B, T = q.shape[:2]
    nc = T // CHUNK

    def to_chunks(a):
        return a.reshape(B, nc, CHUNK, *a.shape[2:]).swapaxes(0, 1)

    state0 = jnp.zeros((B, GLA_HEADS, GLA_DK, GLA_DV), F32)
    s_fin, o = lax.scan(lambda s, inp: gla_block(s, *inp), state0,
                        (to_chunks(q), to_chunks(k), to_chunks(v), to_chunks(log_a)))
    return o.swapaxes(0, 1).reshape(B, T, GLA_HEADS, GLA_DV), s_fin


def diff_attend(q, k, v, mask, lam):
    s = jnp.einsum('bqhmd,bkhmd->bhmqk', q, k).astype(F32) * (DIFF_HD ** -0.5)
    if mask is not None:
        s = jnp.where(mask, s, -jnp.inf)
    p = jax.nn.softmax(s, axis=-1)
    attn = p[:, :, 0] - lam * p[:, :, 1]
    return jnp.einsum('bhqk,bkhe->bqhe', attn, v.astype(F32))


def diff_prompt(q, k, v, lam):
    B, T = q.shape[:2]
    nb = T // Q_BLOCK
    qb = q.reshape(B, nb, Q_BLOCK, *q.shape[2:]).swapaxes(0, 1)
    k_chunk = jnp.arange(T, dtype=jnp.int32) // CHUNK

    def blk(args):
        q_i, i = args
        q_pos = i * Q_BLOCK + jnp.arange(Q_BLOCK, dtype=jnp.int32)
        mask = (q_pos // CHUNK)[:, None] >= k_chunk[None, :]
        return diff_attend(q_i, k, v, mask, lam)

    ob = lax.map(blk, (qb, jnp.arange(nb, dtype=jnp.int32)))
    return ob.swapaxes(0, 1).reshape(B, T, DIFF_HEADS, DIFF_VD)


def out_merge(x, o_gla, o_diff, gg, dg, ma, mb, gla_norm_g, diff_subln_g, lambda_init,
              w_proj_gla, w_proj_diff, w_out, post_norm_g):
    B, T = x.shape[:2]
    yg = rmsnorm(o_gla.astype(x.dtype), gla_norm_g).reshape(B, T, GLA_VAL) * jax.nn.silu(gg)
    yd = (rmsnorm(o_diff.astype(x.dtype), diff_subln_g) * (1.0 - lambda_init)).reshape(B, T, DIFF_VAL) * jax.nn.silu(dg)
    merged = jax.nn.sigmoid(ma) * (yg @ w_proj_gla) + jax.nn.sigmoid(mb) * (yd @ w_proj_diff)
    return x + rmsnorm(merged @ w_out, post_norm_g)


def setup_inputs(seed: int = 0) -> dict:
    key = jax.random.key(seed)
    ks = jax.random.split(key, 20)
    n = jax.random.normal
    return {
        'x_prompt': n(ks[0], (BATCH, SEQ, D_MODEL), F32),
        'x_sample': n(ks[1], (DEC_BATCH, DEC_SEQ, D_MODEL), F32),
        'cache_diff_k': n(ks[2], (DEPTH, DEC_BATCH, PAST_LEN, DIFF_HEADS, DIFF_VD), F32),
        'cache_diff_v': n(ks[3], (DEPTH, DEC_BATCH, PAST_LEN, DIFF_HEADS, DIFF_VD), F32),
        'state_gla': n(ks[4], (DEPTH, DEC_BATCH, GLA_HEADS, GLA_DK, GLA_DV), F32),
        'pre_norm_g': 1.0 + 0.05 * n(ks[5], (DEPTH, D_MODEL), F32),
        'w_in': n(ks[6], (DEPTH, D_MODEL, IN_TOTAL), F32) * D_MODEL ** -0.5,
        'gla_w_a2': n(ks[7], (DEPTH, GLA_RANK, GLA_KEY), F32) * GLA_RANK ** -0.5,
        'gla_b_a': 0.1 * n(ks[8], (DEPTH, GLA_KEY), F32),
        'gla_norm_g': 1.0 + 0.05 * n(ks[9], (DEPTH, GLA_DV), F32),
        'diff_lambda_q1': 0.1 * n(ks[10], (DEPTH, DIFF_HD), F32),
        'diff_lambda_k1': 0.1 * n(ks[11], (DEPTH, DIFF_HD), F32),
        'diff_lambda_q2': 0.1 * n(ks[12], (DEPTH, DIFF_HD), F32),
        'diff_lambda_k2': 0.1 * n(ks[13], (DEPTH, DIFF_HD), F32),
        'diff_subln_g': 1.0 + 0.05 * n(ks[14], (DEPTH, DIFF_VD), F32),
        'w_proj_gla': n(ks[15], (DEPTH, GLA_VAL, D_MODEL), F32) * GLA_VAL ** -0.5,
        'w_proj_diff': n(ks[16], (DEPTH, DIFF_VAL, D_MODEL), F32) * DIFF_VAL ** -0.5,
        'w_out': n(ks[17], (DEPTH, D_MODEL, D_MODEL), F32) * D_MODEL ** -0.5,
        'post_norm_g': 1.0 + 0.05 * n(ks[18], (DEPTH, D_MODEL), F32),
    }


def reference(x_prompt, x_sample, cache_diff_k, cache_diff_v, state_gla, pre_norm_g, w_in,
              gla_w_a2, gla_b_a, gla_norm_g, diff_lambda_q1, diff_lambda_k1, diff_lambda_q2,
              diff_lambda_k2, diff_subln_g, w_proj_gla, w_proj_diff, w_out, post_norm_g):
    hp, hs = x_prompt, x_sample
    kp_l, vp_l, sp_l, ks_l, vs_l, ss_l = [], [], [], [], [], []
    for l in range(DEPTH):
        lambda_init = 0.8 - 0.6 * math.exp(-0.3 * l)
        lam = (jnp.exp(jnp.sum(diff_lambda_q1[l].astype(F32) * diff_lambda_k1[l].astype(F32)))
               - jnp.exp(jnp.sum(diff_lambda_q2[l].astype(F32) * diff_lambda_k2[l].astype(F32)))
               + lambda_init)
        merge_args = (gla_norm_g[l], diff_subln_g[l], lambda_init, w_proj_gla[l], w_proj_diff[l],
                      w_out[l], post_norm_g[l])

        B, T = hp.shape[:2]
        gq, gk, gv, la, gg, dq, dk, dv, dg, ma, mb = in_proj(rmsnorm(hp, pre_norm_g[l]), w_in[l], gla_w_a2[l], gla_b_a[l])
        pos = jnp.arange(T, dtype=jnp.int32)
        dq, dk = rope(dq, pos), rope(dk, pos)
        o_gla, s_p = gla_prompt(gq, gk, gv, la)
        o_diff = diff_prompt(dq, dk, dv, lam)
        hp = out_merge(hp, o_gla, o_diff, gg, dg, ma, mb, *merge_args)
        kp_l.append(dk.reshape(B, T, DIFF_HEADS, DIFF_VD))
        vp_l.append(dv)
        sp_l.append(s_p)

        Bs, Ts = hs.shape[:2]
        gq, gk, gv, la, gg, dq, dk, dv, dg, ma, mb = in_proj(rmsnorm(hs, pre_norm_g[l]), w_in[l], gla_w_a2[l], gla_b_a[l])
        pos_s = PAST_LEN + jnp.arange(Ts, dtype=jnp.int32)
        dq, dk = rope(dq, pos_s), rope(dk, pos_s)
        s_s, o_gla_s = gla_block(state_gla[l].astype(F32), gq, gk, gv, la)
        past_k = cache_diff_k[l].reshape(Bs, -1, DIFF_HEADS, 2, DIFF_HD).astype(dk.dtype)
        k_all = jnp.concatenate([past_k, dk], axis=1)
        v_all = jnp.concatenate([cache_diff_v[l].astype(dv.dtype), dv], axis=1)
        o_diff_s = diff_attend(dq, k_all, v_all, None, lam)
        hs = out_merge(hs, o_gla_s, o_diff_s, gg, dg, ma, mb, *merge_args)
        ks_l.append(dk.reshape(Bs, Ts, DIFF_HEADS, DIFF_VD))
        vs_l.append(dv)
        ss_l.append(s_s)

    return (hp, hs, jnp.stack(kp_l), jnp.stack(vp_l), jnp.stack(sp_l),
            jnp.stack(ks_l), jnp.stack(vs_l), jnp.stack(ss_l))
```

```python
import functools
import math

import jax
import jax.numpy as jnp
from jax import lax
from jax.experimental import pallas as pl
from jax.experimental.pallas import tpu as pltpu

F32 = jnp.float32
BF16 = jnp.bfloat16

D_MODEL = 1024
EPS = 1e-6
CHUNK = 64

GLA_HEADS = 4
GLA_DK = 128
GLA_DV = 256
GLA_KEY = GLA_HEADS * GLA_DK
GLA_VAL = GLA_HEADS * GLA_DV
GLA_RANK = 16
GLA_TAU = 16.0
GLA_SCALE = GLA_DK ** -0.5

DIFF_HEADS = 8
DIFF_HD = 64
DIFF_VD = 128
DIFF_QK = DIFF_HEADS * 2 * DIFF_HD
DIFF_VAL = DIFF_HEADS * DIFF_VD
ROT_DIM = DIFF_HD // 4
ROPE_THETA = 500000.0
QK_SCALE = DIFF_HD ** -0.5

IN_SIZES = (GLA_KEY, GLA_KEY, GLA_VAL, GLA_RANK, GLA_VAL,
            DIFF_QK, DIFF_QK, DIFF_VAL, DIFF_VAL, D_MODEL, D_MODEL)

LANES = 128
VMEM_LIMIT = 56 * 1024 * 1024

_NT = (((1,), (1,)), ((), ()))
_TN = (((0,), (0,)), ((), ()))


def _rms_rows(x, g):
    return x * lax.rsqrt(jnp.mean(x * x, axis=-1, keepdims=True) + EPS) * g


def _params(n_axes):
    return pltpu.CompilerParams(dimension_semantics=("arbitrary",) * n_axes,
                                vmem_limit_bytes=VMEM_LIMIT)


def _const_spec(shape):
    zeros = (0,) * len(shape)
    return pl.BlockSpec(shape, lambda *_: zeros)


def _qkv_body(x_ref, g_ref, w_ref, c_ref, s1_ref, s2_ref, q_ref, k_ref, v_ref):
    xn = _rms_rows(x_ref[0], g_ref[...]).astype(BF16)
    c, s1, s2 = c_ref[...], s1_ref[...], s2_ref[...]

    def rope(t):
        return t * c + pltpu.roll(t, LANES - ROT_DIM // 2, 1) * s1 + pltpu.roll(t, ROT_DIM // 2, 1) * s2

    q = jnp.dot(xn, w_ref[:, 0:DIFF_QK], preferred_element_type=F32)
    for h in range(DIFF_HEADS):
        sl = slice(h * LANES, (h + 1) * LANES)
        q_ref[0, :, sl] = (rope(q[:, sl]) * QK_SCALE).astype(BF16)
    k = jnp.dot(xn, w_ref[:, DIFF_QK:2 * DIFF_QK], preferred_element_type=F32)
    for h in range(DIFF_HEADS):
        sl = slice(h * LANES, (h + 1) * LANES)
        k_ref[0, :, sl] = rope(k[:, sl])
    v_ref[0] = jnp.dot(xn, w_ref[:, 2 * DIFF_QK:], preferred_element_type=F32)


def _qkv_call(x, g, w, tabs, tm):
    B, T, _ = x.shape
    row = pl.BlockSpec((1, tm, D_MODEL), lambda b, i: (b, i, 0))
    tab = pl.BlockSpec((tm, LANES), lambda b, i: (i, 0))
    return pl.pallas_call(
        _qkv_body,
        grid=(B, T // tm),
        in_specs=[row, _const_spec((1, D_MODEL)), _const_spec(w.shape), tab, tab, tab],
        out_specs=[row, row, row],
        out_shape=[jax.ShapeDtypeStruct((B, T, DIFF_QK), BF16),
                   jax.ShapeDtypeStruct((B, T, DIFF_QK), F32),
                   jax.ShapeDtypeStruct((B, T, DIFF_VAL), F32)],
        compiler_params=_params(2),
        name="qkv_proj",
    )(x, g, w, *tabs)


def _rope_tables(pos):
    half = ROT_DIM // 2
    inv = ROPE_THETA ** (-jnp.arange(0, ROT_DIM, 2, dtype=F32) / ROT_DIM)
    ang = pos.astype(F32)[:, None] * inv[None, :]
    cos, sin = jnp.cos(ang), jnp.sin(ang)
    n = pos.shape[0]
    one = jnp.ones((n, DIFF_HD - ROT_DIM), F32)
    zero = jnp.zeros((n, DIFF_HD - ROT_DIM), F32)
    zh = jnp.zeros((n, half), F32)
    c = jnp.concatenate([cos, cos, one], axis=1)
    s1 = jnp.concatenate([-sin, zh, zero], axis=1)
    s2 = jnp.concatenate([zh, sin, zero], axis=1)
    return tuple(jnp.concatenate([t, t], axis=1) for t in (c, s1, s2))


def _gla_body(x_ref, g_ref, w_ref, wa_ref, ba_ref, s0_ref, o_ref, s_ref, st_ref, *, chunk):
    t = pl.program_id(1)
    C = chunk

    @pl.when(t == 0)
    def _():
        for h in range(GLA_HEADS):
            st_ref[h] = s0_ref[0, h].T

    xn = _rms_rows(x_ref[0], g_ref[...]).astype(BF16)
    u = jnp.dot(xn, w_ref[...], preferred_element_type=F32)
    z = jnp.dot(u[:, 2 * GLA_KEY + GLA_VAL:].astype(BF16), wa_ref[...],
                preferred_element_type=F32) + ba_ref[...]
    log_a = (jnp.minimum(z, 0.0) - jnp.log1p(jnp.exp(-jnp.abs(z)))) * (1.0 / GLA_TAU)

    ri = lax.broadcasted_iota(jnp.int32, (C, C), 0)
    ci = lax.broadcasted_iota(jnp.int32, (C, C), 1)
    tril = ri >= ci
    tril_b = jnp.where(tril, 1.0, 0.0).astype(BF16)

    for c in range(x_ref.shape[1] // C):
        rows = slice(c * C, (c + 1) * C)
        la = log_a[rows]
        hi = la.astype(BF16)
        lo = (la - hi.astype(F32)).astype(BF16)
        b = (jnp.dot(tril_b, hi, preferred_element_type=F32)
             + jnp.dot(tril_b, lo, preferred_element_type=F32))
        r = b[C // 2 - 1:C // 2]
        bl = b[C - 1:C]
        e1 = jnp.exp(b - r)
        e2 = jnp.exp(r - b)
        q1 = u[rows, 0:GLA_KEY] * GLA_SCALE * e1
        qs = (q1 * jnp.exp(r)).astype(BF16)
        q1 = q1.astype(BF16)
        k1 = u[rows, GLA_KEY:2 * GLA_KEY] * e2
        kd = (k1 * jnp.exp(bl - r)).astype(BF16)
        k1 = k1.astype(BF16)
        dec = jnp.exp(bl)
        v = u[rows, 2 * GLA_KEY:2 * GLA_KEY + GLA_VAL].astype(BF16)
        for h in range(GLA_HEADS):
            ks = slice(h * GLA_DK, (h + 1) * GLA_DK)
            vs = slice(h * GLA_DV, (h + 1) * GLA_DV)
            a = lax.dot_general(q1[:, ks], k1[:, ks], _NT, preferred_element_type=F32)
            a = jnp.where(tril, a, 0.0).astype(BF16)
            st = st_ref[h]
            o = (jnp.dot(a, v[:, vs], preferred_element_type=F32)
                 + lax.dot_general(qs[:, ks], st.astype(BF16), _NT, preferred_element_type=F32))
            o_ref[0, rows, vs] = o
            st_ref[h] = st * dec[:, ks] + lax.dot_general(v[:, vs], kd[:, ks], _TN,
                                                          preferred_element_type=F32)

    @pl.when(t == pl.num_programs(1) - 1)
    def _():
        for h in range(GLA_HEADS):
            s_ref[0, h] = st_ref[h].T


def _gla_call(x, g, w, wa, ba, s0, tt, chunk):
    B, T, _ = x.shape
    row = pl.BlockSpec((1, tt, D_MODEL), lambda b, i: (b, i, 0))
    state = pl.BlockSpec((1, GLA_HEADS, GLA_DK, GLA_DV), lambda b, i: (b, 0, 0, 0))
    return pl.pallas_call(
        functools.partial(_gla_body, chunk=chunk),
        grid=(B, T // tt),
        in_specs=[row, _const_spec((1, D_MODEL)), _const_spec(w.shape), _const_spec(wa.shape),
                  _const_spec((1, GLA_KEY)), state],
        out_specs=[row, state],
        out_shape=[jax.ShapeDtypeStruct((B, T, GLA_VAL), F32),
                   jax.ShapeDtypeStruct((B, GLA_HEADS, GLA_DK, GLA_DV), F32)],
        scratch_shapes=[pltpu.VMEM((GLA_HEADS, GLA_DV, GLA_DK), F32)],
        compiler_params=_params(2),
        name="gla",
    )(x, g, w, wa, ba, s0)


def _lam(lq1, lk1, lq2, lk2, lambda_init):
    return (jnp.exp(jnp.sum(lq1[...] * lk1[...], axis=-1, keepdims=True))
            - jnp.exp(jnp.sum(lq2[...] * lk2[...], axis=-1, keepdims=True)) + lambda_init)


def _split_maps(q):
    lane = lax.broadcasted_iota(jnp.int32, q.shape, 1)
    zero = jnp.zeros_like(q)
    return jnp.where(lane < DIFF_HD, q, zero), jnp.where(lane >= DIFF_HD, q, zero)


def _attn_body(lq1, lk1, lq2, lk2, q_ref, k_ref, v_ref, o_ref, kb, vb, acc, mrun, *, tq, lambda_init):
    i = pl.program_id(2)
    T = k_ref.shape[1]

    @pl.when(i == 0)
    def _():
        kb[...] = k_ref[0].astype(BF16)
        vb[:, 0:DIFF_VD] = v_ref[0].astype(BF16)
        lane = lax.broadcasted_iota(jnp.int32, (T, LANES), 1)
        vb[:, DIFF_VD:] = jnp.where(lane == 0, 1.0, 0.0).astype(BF16)

    qz = _split_maps(q_ref[0])
    for m in range(2):
        mrun[m] = jnp.full((tq, 1), -jnp.inf, F32)
        acc[m] = jnp.zeros((tq, 2 * DIFF_VD), F32)

    def tile(j, mask):
        start = pl.multiple_of(j * tq, tq)
        kt = kb[pl.ds(start, tq), :]
        vt = vb[pl.ds(start, tq), :]
        for m in range(2):
            s = lax.dot_general(qz[m], kt, _NT, preferred_element_type=F32)
            if mask is not None:
                s = jnp.where(mask, s, -jnp.inf)
            m_old = mrun[m]
            m_new = jnp.maximum(m_old, jnp.max(s, axis=-1, keepdims=True))
            p = jnp.exp(s - m_new).astype(BF16)
            acc[m] = jnp.exp(m_old - m_new) * acc[m] + jnp.dot(p, vt, preferred_element_type=F32)
            mrun[m] = m_new

    def full_tile(j, carry):
        tile(j, None)
        return carry

    lax.fori_loop(0, i, full_tile, 0)
    rc = lax.broadcasted_iota(jnp.int32, (tq, tq), 0) // CHUNK
    cc = lax.broadcasted_iota(jnp.int32, (tq, tq), 1) // CHUNK
    tile(i, rc >= cc)

    a0, a1 = acc[0], acc[1]
    lam = _lam(lq1, lk1, lq2, lk2, lambda_init)
    o_ref[0] = (a0[:, 0:DIFF_VD] / a0[:, DIFF_VD:DIFF_VD + 1]
                - lam * (a1[:, 0:DIFF_VD] / a1[:, DIFF_VD:DIFF_VD + 1]))


def _attn_call(lams, q, k, v, tq, lambda_init):
    B, T, _ = q.shape
    lam_spec = _const_spec((1, DIFF_HD))
    qo = pl.BlockSpec((1, tq, LANES), lambda b, h, i: (b, i, h))
    kv = pl.BlockSpec((1, T, LANES), lambda b, h, i: (b, 0, h))
    return pl.pallas_call(
        functools.partial(_attn_body, tq=tq, lambda_init=lambda_init),
        grid=(B, DIFF_HEADS, T // tq),
        in_specs=[lam_spec] * 4 + [qo, kv, kv],
        out_specs=qo,
        out_shape=jax.ShapeDtypeStruct((B, T, DIFF_VAL), F32),
        scratch_shapes=[pltpu.VMEM((T, LANES), BF16), pltpu.VMEM((T, 2 * DIFF_VD), BF16),
                        pltpu.VMEM((2, tq, 2 * DIFF_VD), F32), pltpu.VMEM((2, tq, 1), F32)],
        compiler_params=_params(3),
        name="diff_attn",
    )(*lams, q, k, v)


def _attn_cache_body(lq1, lk1, lq2, lk2, q_ref, kc_ref, vc_ref, kn_ref, vn_ref, o_ref, *, lambda_init):
    qz = _split_maps(q_ref[0])
    kc, vc = kc_ref[0].astype(BF16), vc_ref[0].astype(BF16)
    kn, vn = kn_ref[0].astype(BF16), vn_ref[0].astype(BF16)
    outs = []
    for m in range(2):
        sc = lax.dot_general(qz[m], kc, _NT, preferred_element_type=F32)
        sn = lax.dot_general(qz[m], kn, _NT, preferred_element_type=F32)
        mx = jnp.maximum(jnp.max(sc, axis=-1, keepdims=True), jnp.max(sn, axis=-1, keepdims=True))
        pc = jnp.exp(sc - mx)
        pn = jnp.exp(sn - mx)
        l = jnp.sum(pc, axis=-1, keepdims=True) + jnp.sum(pn, axis=-1, keepdims=True)
        o = (jnp.dot(pc.astype(BF16), vc, preferred_element_type=F32)
             + jnp.dot(pn.astype(BF16), vn, preferred_element_type=F32))
        outs.append(o / l)
    o_ref[0] = outs[0] - _lam(lq1, lk1, lq2, lk2, lambda_init) * outs[1]


def _attn_cache_call(lams, q, kc, vc, kn, vn, lambda_init):
    B, T, _ = q.shape
    P = kc.shape[1]
    lam_spec = _const_spec((1, DIFF_HD))
    new = pl.BlockSpec((1, T, LANES), lambda b, h: (b, 0, h))
    past = pl.BlockSpec((1, P, LANES), lambda b, h: (b, 0, h))
    return pl.pallas_call(
        functools.partial(_attn_cache_body, lambda_init=lambda_init),
        grid=(B, DIFF_HEADS),
        in_specs=[lam_spec] * 4 + [new, past, past, new, new],
        out_specs=new,
        out_shape=jax.ShapeDtypeStruct((B, T, DIFF_VAL), F32),
        compiler_params=_params(2),
        name="diff_attn_cache",
    )(*lams, q, kc, vc, kn, vn)


def _merge_body(x_ref, og_ref, od_ref, gpre_ref, wg_ref, ggla_ref, gsub_ref, wpg_ref, wpd_ref, wo_ref,
                gpost_ref, y_ref, yg_s, yd_s, *, lambda_init):
    x = x_ref[...]
    xn = _rms_rows(x, gpre_ref[...]).astype(BF16)
    gates = jnp.dot(xn, wg_ref[...], preferred_element_type=F32)
    og, od = og_ref[...], od_ref[...]
    for h in range(GLA_HEADS):
        sl = slice(h * GLA_DV, (h + 1) * GLA_DV)
        yg_s[:, sl] = (_rms_rows(og[:, sl], ggla_ref[...]) * jax.nn.silu(gates[:, sl])).astype(BF16)
    for h in range(DIFF_HEADS):
        sl = slice(h * DIFF_VD, (h + 1) * DIFF_VD)
        dg = gates[:, GLA_VAL + h * DIFF_VD:GLA_VAL + (h + 1) * DIFF_VD]
        yd_s[:, sl] = (_rms_rows(od[:, sl], gsub_ref[...]) * (1.0 - lambda_init) * jax.nn.silu(dg)).astype(BF16)
    ma = gates[:, GLA_VAL + DIFF_VAL:GLA_VAL + DIFF_VAL + D_MODEL]
    mb = gates[:, GLA_VAL + DIFF_VAL + D_MODEL:]
    merged = (jax.nn.sigmoid(ma) * jnp.dot(yg_s[...], wpg_ref[...], preferred_element_type=F32)
              + jax.nn.sigmoid(mb) * jnp.dot(yd_s[...], wpd_ref[...], preferred_element_type=F32))
    z = jnp.dot(merged.astype(BF16), wo_ref[...], preferred_element_type=F32)
    y_ref[...] = x + _rms_rows(z, gpost_ref[...])


def _merge_call(x, og, od, gpre, wg, ggla, gsub, wpg, wpd, wo, gpost, tm, lambda_init):
    N = x.shape[0]
    row = pl.BlockSpec((tm, D_MODEL), lambda i: (i, 0))
    return pl.pallas_call(
        functools.partial(_merge_body, lambda_init=lambda_init),
        grid=(N // tm,),
        in_specs=[row, row, row, _const_spec((1, D_MODEL)), _const_spec(wg.shape), _const_spec((1, GLA_DV)),
                  _const_spec((1, DIFF_VD)), _const_spec(wpg.shape), _const_spec(wpd.shape),
                  _const_spec(wo.shape), _const_spec((1, D_MODEL))],
        out_specs=row,
        out_shape=jax.ShapeDtypeStruct((N, D_MODEL), F32),
        scratch_shapes=[pltpu.VMEM((tm, GLA_VAL), BF16), pltpu.VMEM((tm, DIFF_VAL), BF16)],
        compiler_params=_params(1),
        name="merge",
    )(x, og, od, gpre, wg, ggla, gsub, wpg, wpd, wo, gpost)


def _tile(n, target):
    t = min(n, target)
    while n % t:
        t -= 1
    return t


def kernel(x_prompt, x_sample, cache_diff_k, cache_diff_v, state_gla, pre_norm_g, w_in, gla_w_a2, gla_b_a,
           gla_norm_g, diff_lambda_q1, diff_lambda_k1, diff_lambda_q2, diff_lambda_k2, diff_subln_g,
           w_proj_gla, w_proj_diff, w_out, post_norm_g):
    depth = w_in.shape[0]
    past_len = cache_diff_k.shape[2]
    hp, hs = x_prompt, x_sample
    outs = [[] for _ in range(6)]
    offs = [0]
    for s in IN_SIZES:
        offs.append(offs[-1] + s)
    for l in range(depth):
        lambda_init = 0.8 - 0.6 * math.exp(-0.3 * l)
        w = w_in[l]
        seg = [w[:, offs[i]:offs[i + 1]].astype(BF16) for i in range(len(IN_SIZES))]
        gq, gk, gv, gr, gg, dq, dk, dv, dg, ma, mb = seg
        w_gla = jnp.concatenate([gq, gk, gv, jnp.pad(gr, ((0, 0), (0, LANES - GLA_RANK)))], axis=1)
        w_qkv = jnp.concatenate([dq, dk, dv], axis=1)
        w_gate = jnp.concatenate([gg, dg, ma, mb], axis=1)
        wa = jnp.pad(gla_w_a2[l], ((0, LANES - GLA_RANK), (0, 0))).astype(BF16)
        ba = gla_b_a[l][None]
        gpre = pre_norm_g[l][None]
        lams = (diff_lambda_q1[l][None], diff_lambda_k1[l][None], diff_lambda_q2[l][None], diff_lambda_k2[l][None])
        merge_w = (gpre, w_gate, gla_norm_g[l][None], diff_subln_g[l][None], w_proj_gla[l].astype(BF16),
                   w_proj_diff[l].astype(BF16), w_out[l].astype(BF16), post_norm_g[l][None])

        B, T, _ = hp.shape
        q, k, v = _qkv_call(hp, gpre, w_qkv, _rope_tables(jnp.arange(T, dtype=jnp.int32)), _tile(T, 512))
        chunk = _tile(T, 64)
        o_gla, s_p = _gla_call(hp, gpre, w_gla, wa, ba, jnp.zeros((B, GLA_HEADS, GLA_DK, GLA_DV), F32),
                               _tile(T, 256), chunk)
        o_diff = _attn_call(lams, q, k, v, _tile(T, 256), lambda_init)
        hp = _merge_call(hp.reshape(B * T, D_MODEL), o_gla.reshape(B * T, GLA_VAL),
                         o_diff.reshape(B * T, DIFF_VAL), *merge_w, _tile(B * T, 256),
                         lambda_init).reshape(B, T, D_MODEL)
        outs[0].append(k.reshape(B, T, DIFF_HEADS, DIFF_VD))
        outs[1].append(v.reshape(B, T, DIFF_HEADS, DIFF_VD))
        outs[2].append(s_p)

        Bs, Ts, _ = hs.shape
        pos_s = past_len + jnp.arange(Ts, dtype=jnp.int32)
        q, k, v = _qkv_call(hs, gpre, w_qkv, _rope_tables(pos_s), Ts)
        o_gla, s_s = _gla_call(hs, gpre, w_gla, wa, ba, state_gla[l].astype(F32), Ts, Ts)
        o_diff = _attn_cache_call(lams, q, cache_diff_k[l].reshape(Bs, past_len, DIFF_QK),
                                  cache_diff_v[l].reshape(Bs, past_len, DIFF_VAL), k, v, lambda_init)
        hs = _merge_call(hs.reshape(Bs * Ts, D_MODEL), o_gla.reshape(Bs * Ts, GLA_VAL),
                         o_diff.reshape(Bs * Ts, DIFF_VAL), *merge_w, _tile(Bs * Ts, 256),
                         lambda_init).reshape(Bs, Ts, D_MODEL)
        outs[3].append(k.reshape(Bs, Ts, DIFF_HEADS, DIFF_VD))
        outs[4].append(v.reshape(Bs, Ts, DIFF_HEADS, DIFF_VD))
        outs[5].append(s_s)

    return (hp, hs, jnp.stack(outs[0]), jnp.stack(outs[1]), jnp.stack(outs[2]),
            jnp.stack(outs[3]), jnp.stack(outs[4]), jnp.stack(outs[5]))
```

```python
import functools
import math

import jax
import jax.numpy as jnp
from jax import lax
from jax.experimental import pallas as pl
from jax.experimental.pallas import tpu as pltpu

F32 = jnp.float32
BF16 = jnp.bfloat16

D_MODEL = 1024
EPS = 1e-6
CHUNK = 64

GLA_HEADS = 4
GLA_DK = 128
GLA_DV = 256
GLA_KEY = GLA_HEADS * GLA_DK
GLA_VAL = GLA_HEADS * GLA_DV
GLA_RANK = 16
GLA_TAU = 16.0
GLA_SCALE = GLA_DK ** -0.5

DIFF_HEADS = 8
DIFF_HD = 64
DIFF_VD = 128
DIFF_QK = DIFF_HEADS * 2 * DIFF_HD
DIFF_VAL = DIFF_HEADS * DIFF_VD
ROT_DIM = DIFF_HD // 4
ROPE_THETA = 500000.0
QK_SCALE = DIFF_HD ** -0.5
Q_PRESCALE = QK_SCALE * math.log2(math.e)

IN_SIZES = (GLA_KEY, GLA_KEY, GLA_VAL, GLA_RANK, GLA_VAL,
            DIFF_QK, DIFF_QK, DIFF_VAL, DIFF_VAL, D_MODEL, D_MODEL)

LANES = 128
VMEM_LIMIT = 56 * 1024 * 1024

_NT = (((1,), (1,)), ((), ()))
_TN = (((0,), (0,)), ((), ()))


def _rms_rows(x, g):
    return x * lax.rsqrt(jnp.mean(x * x, axis=-1, keepdims=True) + EPS) * g


def _params(n_axes):
    return pltpu.CompilerParams(dimension_semantics=("arbitrary",) * n_axes,
                                vmem_limit_bytes=VMEM_LIMIT)


def _const_spec(shape):
    zeros = (0,) * len(shape)
    return pl.BlockSpec(shape, lambda *_: zeros)


def _qkv_body(x_ref, g_ref, w_ref, c_ref, s1_ref, s2_ref, k_ref, v_ref, q16_ref, k16_ref, v16_ref):
    xn = _rms_rows(x_ref[0], g_ref[...]).astype(BF16)
    c, s1, s2 = c_ref[...], s1_ref[...], s2_ref[...]

    def rope(t):
        return t * c + pltpu.roll(t, LANES - ROT_DIM // 2, 1) * s1 + pltpu.roll(t, ROT_DIM // 2, 1) * s2

    q = jnp.dot(xn, w_ref[:, 0:DIFF_QK], preferred_element_type=F32)
    for h in range(DIFF_HEADS):
        sl = slice(h * LANES, (h + 1) * LANES)
        q16_ref[0, h] = (rope(q[:, sl]) * Q_PRESCALE).astype(BF16)
    k = jnp.dot(xn, w_ref[:, DIFF_QK:2 * DIFF_QK], preferred_element_type=F32)
    for h in range(DIFF_HEADS):
        sl = slice(h * LANES, (h + 1) * LANES)
        kr = rope(k[:, sl])
        k_ref[0, :, sl] = kr
        k16_ref[0, h] = kr.astype(BF16)
    v = jnp.dot(xn, w_ref[:, 2 * DIFF_QK:], preferred_element_type=F32)
    v_ref[0] = v
    for h in range(DIFF_HEADS):
        v16_ref[0, h] = v[:, h * LANES:(h + 1) * LANES].astype(BF16)


def _qkv_call(x, g, w, tabs, tm):
    B, T, _ = x.shape
    row = pl.BlockSpec((1, tm, D_MODEL), lambda b, i: (b, i, 0))
    heads = pl.BlockSpec((1, DIFF_HEADS, tm, LANES), lambda b, i: (b, 0, i, 0))
    tab = pl.BlockSpec((tm, LANES), lambda b, i: (i, 0))
    head_major = jax.ShapeDtypeStruct((B, DIFF_HEADS, T, LANES), BF16)
    return pl.pallas_call(
        _qkv_body,
        grid=(B, T // tm),
        in_specs=[row, _const_spec((1, D_MODEL)), _const_spec(w.shape), tab, tab, tab],
        out_specs=[row, row, heads, heads, heads],
        out_shape=[jax.ShapeDtypeStruct((B, T, DIFF_QK), F32),
                   jax.ShapeDtypeStruct((B, T, DIFF_VAL), F32),
                   head_major, head_major, head_major],
        compiler_params=_params(2),
        name="qkv_proj",
    )(x, g, w, *tabs)


def _rope_tables(pos):
    half = ROT_DIM // 2
    inv = ROPE_THETA ** (-jnp.arange(0, ROT_DIM, 2, dtype=F32) / ROT_DIM)
    ang = pos.astype(F32)[:, None] * inv[None, :]
    cos, sin = jnp.cos(ang), jnp.sin(ang)
    n = pos.shape[0]
    one = jnp.ones((n, DIFF_HD - ROT_DIM), F32)
    zero = jnp.zeros((n, DIFF_HD - ROT_DIM), F32)
    zh = jnp.zeros((n, half), F32)
    c = jnp.concatenate([cos, cos, one], axis=1)
    s1 = jnp.concatenate([-sin, zh, zero], axis=1)
    s2 = jnp.concatenate([zh, sin, zero], axis=1)
    return tuple(jnp.concatenate([t, t], axis=1) for t in (c, s1, s2))


def _gla_body(x_ref, g_ref, w_ref, wa_ref, ba_ref, s0_ref, o_ref, s_ref, st_ref, *, chunk):
    t = pl.program_id(1)
    C = chunk

    @pl.when(t == 0)
    def _():
        for h in range(GLA_HEADS):
            st_ref[h] = s0_ref[0, h].T

    xn = _rms_rows(x_ref[0], g_ref[...]).astype(BF16)
    u = jnp.dot(xn, w_ref[...], preferred_element_type=F32)
    z = jnp.dot(u[:, 2 * GLA_KEY + GLA_VAL:].astype(BF16), wa_ref[...],
                preferred_element_type=F32) + ba_ref[...]
    log_a = (jnp.minimum(z, 0.0) - jnp.log1p(jnp.exp(-jnp.abs(z)))) * (1.0 / GLA_TAU)

    ri = lax.broadcasted_iota(jnp.int32, (C, C), 0)
    ci = lax.broadcasted_iota(jnp.int32, (C, C), 1)
    tril = ri >= ci
    tril_b = jnp.where(tril, 1.0, 0.0).astype(BF16)

    for c in range(x_ref.shape[1] // C):
        rows = slice(c * C, (c + 1) * C)
        la = log_a[rows]
        hi = la.astype(BF16)
        lo = (la - hi.astype(F32)).astype(BF16)
        b = (jnp.dot(tril_b, hi, preferred_element_type=F32)
             + jnp.dot(tril_b, lo, preferred_element_type=F32))
        r = b[C // 2 - 1:C // 2]
        bl = b[C - 1:C]
        e1 = jnp.exp(b - r)
        e2 = jnp.exp(r - b)
        q1 = u[rows, 0:GLA_KEY] * GLA_SCALE * e1
        qs = (q1 * jnp.exp(r)).astype(BF16)
        q1 = q1.astype(BF16)
        k1 = u[rows, GLA_KEY:2 * GLA_KEY] * e2
        kd = (k1 * jnp.exp(bl - r)).astype(BF16)
        k1 = k1.astype(BF16)
        dec = jnp.exp(bl)
        v = u[rows, 2 * GLA_KEY:2 * GLA_KEY + GLA_VAL].astype(BF16)
        for h in range(GLA_HEADS):
            ks = slice(h * GLA_DK, (h + 1) * GLA_DK)
            vs = slice(h * GLA_DV, (h + 1) * GLA_DV)
            a = lax.dot_general(q1[:, ks], k1[:, ks], _NT, preferred_element_type=F32)
            a = jnp.where(tril, a, 0.0).astype(BF16)
            st = st_ref[h]
            o = (jnp.dot(a, v[:, vs], preferred_element_type=F32)
                 + lax.dot_general(qs[:, ks], st.astype(BF16), _NT, preferred_element_type=F32))
            o_ref[0, rows, vs] = o
            st_ref[h] = st * dec[:, ks] + lax.dot_general(v[:, vs], kd[:, ks], _TN,
                                                          preferred_element_type=F32)

    @pl.when(t == pl.num_programs(1) - 1)
    def _():
        for h in range(GLA_HEADS):
            s_ref[0, h] = st_ref[h].T


def _gla_call(x, g, w, wa, ba, s0, layer, tt, chunk):
    B, T, _ = x.shape
    row = pl.BlockSpec((1, tt, D_MODEL), lambda b, i: (b, i, 0))
    state_shape = (1, GLA_HEADS, GLA_DK, GLA_DV)
    return pl.pallas_call(
        functools.partial(_gla_body, chunk=chunk),
        grid=(B, T // tt),
        in_specs=[row, _const_spec((1, D_MODEL)), _const_spec(w.shape), _const_spec(wa.shape),
                  _const_spec((1, GLA_KEY)), pl.BlockSpec(state_shape, lambda b, i: (layer * B + b, 0, 0, 0))],
        out_specs=[row, pl.BlockSpec(state_shape, lambda b, i: (b, 0, 0, 0))],
        out_shape=[jax.ShapeDtypeStruct((B, T, GLA_VAL), F32),
                   jax.ShapeDtypeStruct((B, GLA_HEADS, GLA_DK, GLA_DV), F32)],
        scratch_shapes=[pltpu.VMEM((GLA_HEADS, GLA_DV, GLA_DK), F32)],
        compiler_params=_params(2),
        name="gla",
    )(x, g, w, wa, ba, s0)


def _lam(lq1, lk1, lq2, lk2, lambda_init):
    return (jnp.exp(jnp.sum(lq1[...] * lk1[...], axis=-1, keepdims=True))
            - jnp.exp(jnp.sum(lq2[...] * lk2[...], axis=-1, keepdims=True)) + lambda_init)


def _split_maps(q):
    lane = lax.broadcasted_iota(jnp.int32, q.shape, 1)
    zero = jnp.zeros_like(q)
    return jnp.where(lane < DIFF_HD, q, zero), jnp.where(lane >= DIFF_HD, q, zero)


def _attn_body(lq1, lk1, lq2, lk2, q_ref, k_ref, v_ref, o_ref, vb, s_ref, mx_ref, m_ref, acc_ref, *,
               tq, lambda_init):
    i = pl.program_id(2)
    T = k_ref.shape[2]
    nblk = tq // LANES

    @pl.when(i == 0)
    def _():
        vb[:, 0:DIFF_VD] = v_ref[0, 0]
        lane = lax.broadcasted_iota(jnp.int32, (T, LANES), 1)
        vb[:, DIFF_VD:] = jnp.where(lane == 0, 1.0, 0.0).astype(BF16)

    qz = _split_maps(q_ref[0, 0])
    for m in range(2):
        mx_ref[m] = jnp.full((tq, LANES), -jnp.inf, F32)

    def scores(j, mask):
        start = pl.multiple_of(j * tq, tq)
        kt = k_ref[0, 0, pl.ds(start, tq), :]
        for m in range(2):
            s = lax.dot_general(qz[m], kt, _NT, preferred_element_type=F32)
            if mask is not None:
                s = jnp.where(mask, s, -jnp.inf)
            s_ref[m, j] = s
            blk = [s[:, c * LANES:(c + 1) * LANES] for c in range(nblk)]
            mx_ref[m] = jnp.maximum(mx_ref[m], functools.reduce(jnp.maximum, blk))

    def full_tile(j, carry):
        scores(j, None)
        return carry

    lax.fori_loop(0, i, full_tile, 0)
    rc = lax.broadcasted_iota(jnp.int32, (tq, tq), 0) // CHUNK
    cc = lax.broadcasted_iota(jnp.int32, (tq, tq), 1) // CHUNK
    scores(i, rc >= cc)

    for m in range(2):
        m_ref[m] = jnp.broadcast_to(jnp.max(mx_ref[m], axis=-1, keepdims=True), (tq, LANES))
        acc_ref[m] = jnp.zeros((tq, 2 * DIFF_VD), F32)

    def weighted(j, carry):
        start = pl.multiple_of(j * tq, tq)
        vt = vb[pl.ds(start, tq), :]
        for m in range(2):
            mt = m_ref[m]
            p = jnp.exp2(s_ref[m, j] - jnp.concatenate([mt] * nblk, axis=1)).astype(BF16)
            acc_ref[m] += jnp.dot(p, vt, preferred_element_type=F32)
        return carry

    lax.fori_loop(0, i + 1, weighted, 0)

    a0, a1 = acc_ref[0], acc_ref[1]
    lam = _lam(lq1, lk1, lq2, lk2, lambda_init)
    o_ref[0] = (a0[:, 0:DIFF_VD] / a0[:, DIFF_VD:DIFF_VD + 1]
                - lam * (a1[:, 0:DIFF_VD] / a1[:, DIFF_VD:DIFF_VD + 1]))


def _attn_call(lams, q, k, v, tq, lambda_init):
    B, _, T, _ = q.shape
    nq = T // tq
    lam_spec = _const_spec((1, DIFF_HD))
    qs = pl.BlockSpec((1, 1, tq, LANES), lambda b, h, i: (b, h, i, 0))
    kv = pl.BlockSpec((1, 1, T, LANES), lambda b, h, i: (b, h, 0, 0))
    return pl.pallas_call(
        functools.partial(_attn_body, tq=tq, lambda_init=lambda_init),
        grid=(B, DIFF_HEADS, nq),
        in_specs=[lam_spec] * 4 + [qs, kv, kv],
        out_specs=pl.BlockSpec((1, tq, LANES), lambda b, h, i: (b, i, h)),
        out_shape=jax.ShapeDtypeStruct((B, T, DIFF_VAL), F32),
        scratch_shapes=[pltpu.VMEM((T, 2 * DIFF_VD), BF16),
                        pltpu.VMEM((2, nq, tq, tq), F32),
                        pltpu.VMEM((2, tq, LANES), F32),
                        pltpu.VMEM((2, tq, LANES), F32),
                        pltpu.VMEM((2, tq, 2 * DIFF_VD), F32)],
        compiler_params=_params(3),
        name="diff_attn",
    )(*lams, q, k, v)


def _attn_cache_body(lq1, lk1, lq2, lk2, q_ref, kc_ref, vc_ref, kn_ref, vn_ref, o_ref, *, lambda_init):
    qz = _split_maps(q_ref[0, 0])
    kc, vc = kc_ref[0].astype(BF16), vc_ref[0].astype(BF16)
    kn, vn = kn_ref[0, 0], vn_ref[0, 0]
    outs = []
    for m in range(2):
        sc = lax.dot_general(qz[m], kc, _NT, preferred_element_type=F32)
        sn = lax.dot_general(qz[m], kn, _NT, preferred_element_type=F32)
        mx = jnp.maximum(jnp.max(sc, axis=-1, keepdims=True), jnp.max(sn, axis=-1, keepdims=True))
        pc = jnp.exp2(sc - mx)
        pn = jnp.exp2(sn - mx)
        l = jnp.sum(pc, axis=-1, keepdims=True) + jnp.sum(pn, axis=-1, keepdims=True)
        o = (jnp.dot(pc.astype(BF16), vc, preferred_element_type=F32)
             + jnp.dot(pn.astype(BF16), vn, preferred_element_type=F32))
        outs.append(o / l)
    o_ref[0] = outs[0] - _lam(lq1, lk1, lq2, lk2, lambda_init) * outs[1]


def _attn_cache_call(lams, q, kc, vc, kn, vn, layer, lambda_init):
    B, _, T, _ = q.shape
    P = kc.shape[1]
    lam_spec = _const_spec((1, DIFF_HD))
    new = pl.BlockSpec((1, 1, T, LANES), lambda b, h: (b, h, 0, 0))
    past = pl.BlockSpec((1, P, LANES), lambda b, h: (layer * B + b, 0, h))
    return pl.pallas_call(
        functools.partial(_attn_cache_body, lambda_init=lambda_init),
        grid=(B, DIFF_HEADS),
        in_specs=[lam_spec] * 4 + [new, past, past, new, new],
        out_specs=pl.BlockSpec((1, T, LANES), lambda b, h: (b, 0, h)),
        out_shape=jax.ShapeDtypeStruct((B, T, DIFF_VAL), F32),
        compiler_params=_params(2),
        name="diff_attn_cache",
    )(*lams, q, kc, vc, kn, vn)


def _merge_body(x_ref, og_ref, od_ref, gpre_ref, wg_ref, ggla_ref, gsub_ref, wpg_ref, wpd_ref, wo_ref,
                gpost_ref, y_ref, yg_s, yd_s, *, lambda_init):
    x = x_ref[...]
    xn = _rms_rows(x, gpre_ref[...]).astype(BF16)
    gates = jnp.dot(xn, wg_ref[...], preferred_element_type=F32)
    og, od = og_ref[...], od_ref[...]
    for h in range(GLA_HEADS):
        sl = slice(h * GLA_DV, (h + 1) * GLA_DV)
        yg_s[:, sl] = (_rms_rows(og[:, sl], ggla_ref[...]) * jax.nn.silu(gates[:, sl])).astype(BF16)
    for h in range(DIFF_HEADS):
        sl = slice(h * DIFF_VD, (h + 1) * DIFF_VD)
        dg = gates[:, GLA_VAL + h * DIFF_VD:GLA_VAL + (h + 1) * DIFF_VD]
        yd_s[:, sl] = (_rms_rows(od[:, sl], gsub_ref[...]) * (1.0 - lambda_init) * jax.nn.silu(dg)).astype(BF16)
    ma = gates[:, GLA_VAL + DIFF_VAL:GLA_VAL + DIFF_VAL + D_MODEL]
    mb = gates[:, GLA_VAL + DIFF_VAL + D_MODEL:]
    merged = (jax.nn.sigmoid(ma) * jnp.dot(yg_s[...], wpg_ref[...], preferred_element_type=F32)
              + jax.nn.sigmoid(mb) * jnp.dot(yd_s[...], wpd_ref[...], preferred_element_type=F32))
    z = jnp.dot(merged.astype(BF16), wo_ref[...], preferred_element_type=F32)
    y_ref[...] = x + _rms_rows(z, gpost_ref[...])


def _merge_call(x, og, od, gpre, wg, ggla, gsub, wpg, wpd, wo, gpost, tm, lambda_init):
    N = x.shape[0]
    row = pl.BlockSpec((tm, D_MODEL), lambda i: (i, 0))
    return pl.pallas_call(
        functools.partial(_merge_body, lambda_init=lambda_init),
        grid=(N // tm,),
        in_specs=[row, row, row, _const_spec((1, D_MODEL)), _const_spec(wg.shape), _const_spec((1, GLA_DV)),
                  _const_spec((1, DIFF_VD)), _const_spec(wpg.shape), _const_spec(wpd.shape),
                  _const_spec(wo.shape), _const_spec((1, D_MODEL))],
        out_specs=row,
        out_shape=jax.ShapeDtypeStruct((N, D_MODEL), F32),
        scratch_shapes=[pltpu.VMEM((tm, GLA_VAL), BF16), pltpu.VMEM((tm, DIFF_VAL), BF16)],
        compiler_params=_params(1),
        name="merge",
    )(x, og, od, gpre, wg, ggla, gsub, wpg, wpd, wo, gpost)


def _tile(n, target):
    t = min(n, target)
    while n % t:
        t -= 1
    return t


def kernel(x_prompt, x_sample, cache_diff_k, cache_diff_v, state_gla, pre_norm_g, w_in, gla_w_a2, gla_b_a,
           gla_norm_g, diff_lambda_q1, diff_lambda_k1, diff_lambda_q2, diff_lambda_k2, diff_subln_g,
           w_proj_gla, w_proj_diff, w_out, post_norm_g):
    depth = w_in.shape[0]
    past_len = cache_diff_k.shape[2]
    hp, hs = x_prompt, x_sample
    B, T, _ = hp.shape
    Bs, Ts, _ = hs.shape
    cache_k = cache_diff_k.reshape(depth * Bs, past_len, DIFF_QK)
    cache_v = cache_diff_v.reshape(depth * Bs, past_len, DIFF_VAL)
    state_in = state_gla.astype(F32).reshape(depth * Bs, GLA_HEADS, GLA_DK, GLA_DV)
    zero_state = jnp.zeros((B, GLA_HEADS, GLA_DK, GLA_DV), F32)
    outs = [[] for _ in range(6)]
    offs = [0]
    for s in IN_SIZES:
        offs.append(offs[-1] + s)
    for l in range(depth):
        lambda_init = 0.8 - 0.6 * math.exp(-0.3 * l)
        w = w_in[l]
        seg = [w[:, offs[i]:offs[i + 1]].astype(BF16) for i in range(len(IN_SIZES))]
        gq, gk, gv, gr, gg, dq, dk, dv, dg, ma, mb = seg
        w_gla = jnp.concatenate([gq, gk, gv, jnp.pad(gr, ((0, 0), (0, LANES - GLA_RANK)))], axis=1)
        w_qkv = jnp.concatenate([dq, dk, dv], axis=1)
        w_gate = jnp.concatenate([gg, dg, ma, mb], axis=1)
        wa = jnp.pad(gla_w_a2[l], ((0, LANES - GLA_RANK), (0, 0))).astype(BF16)
        ba = gla_b_a[l][None]
        gpre = pre_norm_g[l][None]
        lams = (diff_lambda_q1[l][None], diff_lambda_k1[l][None], diff_lambda_q2[l][None], diff_lambda_k2[l][None])
        merge_w = (gpre, w_gate, gla_norm_g[l][None], diff_subln_g[l][None], w_proj_gla[l].astype(BF16),
                   w_proj_diff[l].astype(BF16), w_out[l].astype(BF16), post_norm_g[l][None])

        k, v, q16, k16, v16 = _qkv_call(hp, gpre, w_qkv, _rope_tables(jnp.arange(T, dtype=jnp.int32)),
                                        _tile(T, 512))
        o_gla, s_p = _gla_call(hp, gpre, w_gla, wa, ba, zero_state, 0, _tile(T, 256), _tile(T, 64))
        o_diff = _attn_call(lams, q16, k16, v16, _tile(T, 512), lambda_init)
        hp = _merge_call(hp.reshape(B * T, D_MODEL), o_gla.reshape(B * T, GLA_VAL),
                         o_diff.reshape(B * T, DIFF_VAL), *merge_w, _tile(B * T, 256),
                         lambda_init).reshape(B, T, D_MODEL)
        outs[0].append(k.reshape(B, T, DIFF_HEADS, DIFF_VD))
        outs[1].append(v.reshape(B, T, DIFF_HEADS, DIFF_VD))
        outs[2].append(s_p)

        pos_s = past_len + jnp.arange(Ts, dtype=jnp.int32)
        k, v, q16, k16, v16 = _qkv_call(hs, gpre, w_qkv, _rope_tables(pos_s), Ts)
        o_gla, s_s = _gla_call(hs, gpre, w_gla, wa, ba, state_in, l, Ts, Ts)
        o_diff = _attn_cache_call(lams, q16, cache_k, cache_v, k16, v16, l, lambda_init)
        hs = _merge_call(hs.reshape(Bs * Ts, D_MODEL), o_gla.reshape(Bs * Ts, GLA_VAL),
                         o_diff.reshape(Bs * Ts, DIFF_VAL), *merge_w, _tile(Bs * Ts, 256),
                         lambda_init).reshape(Bs, Ts, D_MODEL)
        outs[3].append(k.reshape(Bs, Ts, DIFF_HEADS, DIFF_VD))
        outs[4].append(v.reshape(Bs, Ts, DIFF_HEADS, DIFF_VD))
        outs[5].append(s_s)

    return (hp, hs, jnp.stack(outs[0]), jnp.stack(outs[1]), jnp.stack(outs[2]),
            jnp.stack(outs[3]), jnp.stack(outs[4]), jnp.stack(outs[5]))
```

```python
import functools
import math

import jax
import jax.numpy as jnp
from jax import lax
from jax.experimental import pallas as pl
from jax.experimental.pallas import tpu as pltpu

F32 = jnp.float32
BF16 = jnp.bfloat16

D_MODEL = 1024
EPS = 1e-6
CHUNK = 64

GLA_HEADS = 4
GLA_DK = 128
GLA_DV = 256
GLA_KEY = GLA_HEADS * GLA_DK
GLA_VAL = GLA_HEADS * GLA_DV
GLA_RANK = 16
GLA_TAU = 16.0
GLA_SCALE = GLA_DK ** -0.5

DIFF_HEADS = 8
DIFF_HD = 64
DIFF_VD = 128
DIFF_QK = DIFF_HEADS * 2 * DIFF_HD
DIFF_VAL = DIFF_HEADS * DIFF_VD
ROT_DIM = DIFF_HD // 4
ROPE_THETA = 500000.0
QK_SCALE = DIFF_HD ** -0.5
Q_PRESCALE = QK_SCALE * math.log2(math.e)

IN_SIZES = (GLA_KEY, GLA_KEY, GLA_VAL, GLA_RANK, GLA_VAL,
            DIFF_QK, DIFF_QK, DIFF_VAL, DIFF_VAL, D_MODEL, D_MODEL)

LANES = 128
VMEM_LIMIT = 56 * 1024 * 1024

_NT = (((1,), (1,)), ((), ()))
_TN = (((0,), (0,)), ((), ()))


def _rms_rows(x, g):
    return x * lax.rsqrt(jnp.mean(x * x, axis=-1, keepdims=True) + EPS) * g


def _params(n_axes):
    return pltpu.CompilerParams(dimension_semantics=("arbitrary",) * n_axes,
                                vmem_limit_bytes=VMEM_LIMIT)


def _const_spec(shape):
    zeros = (0,) * len(shape)
    return pl.BlockSpec(shape, lambda *_: zeros)


def _qkv_body(x_ref, g_ref, w_ref, c_ref, s1_ref, s2_ref, k_ref, v_ref, q16_ref, k16_ref, v16_ref):
    xn = _rms_rows(x_ref[0], g_ref[...]).astype(BF16)
    c, s1, s2 = c_ref[...], s1_ref[...], s2_ref[...]

    def rope(t):
        return t * c + pltpu.roll(t, LANES - ROT_DIM // 2, 1) * s1 + pltpu.roll(t, ROT_DIM // 2, 1) * s2

    q = jnp.dot(xn, w_ref[:, 0:DIFF_QK], preferred_element_type=F32)
    for h in range(DIFF_HEADS):
        sl = slice(h * LANES, (h + 1) * LANES)
        q16_ref[0, h] = (rope(q[:, sl]) * Q_PRESCALE).astype(BF16)
    k = jnp.dot(xn, w_ref[:, DIFF_QK:2 * DIFF_QK], preferred_element_type=F32)
    for h in range(DIFF_HEADS):
        sl = slice(h * LANES, (h + 1) * LANES)
        kr = rope(k[:, sl])
        k_ref[0, :, sl] = kr
        k16_ref[0, h] = kr.astype(BF16)
    v = jnp.dot(xn, w_ref[:, 2 * DIFF_QK:], preferred_element_type=F32)
    v_ref[0] = v
    for h in range(DIFF_HEADS):
        v16_ref[0, h] = v[:, h * LANES:(h + 1) * LANES].astype(BF16)


def _qkv_call(x, g, w, tabs, tm):
    B, T, _ = x.shape
    row = pl.BlockSpec((1, tm, D_MODEL), lambda b, i: (b, i, 0))
    heads = pl.BlockSpec((1, DIFF_HEADS, tm, LANES), lambda b, i: (b, 0, i, 0))
    tab = pl.BlockSpec((tm, LANES), lambda b, i: (i, 0))
    head_major = jax.ShapeDtypeStruct((B, DIFF_HEADS, T, LANES), BF16)
    return pl.pallas_call(
        _qkv_body,
        grid=(B, T // tm),
        in_specs=[row, _const_spec((1, D_MODEL)), _const_spec(w.shape), tab, tab, tab],
        out_specs=[row, row, heads, heads, heads],
        out_shape=[jax.ShapeDtypeStruct((B, T, DIFF_QK), F32),
                   jax.ShapeDtypeStruct((B, T, DIFF_VAL), F32),
                   head_major, head_major, head_major],
        compiler_params=_params(2),
        name="qkv_proj",
    )(x, g, w, *tabs)


def _rope_tables(pos):
    half = ROT_DIM // 2
    inv = ROPE_THETA ** (-jnp.arange(0, ROT_DIM, 2, dtype=F32) / ROT_DIM)
    ang = pos.astype(F32)[:, None] * inv[None, :]
    cos, sin = jnp.cos(ang), jnp.sin(ang)
    n = pos.shape[0]
    one = jnp.ones((n, DIFF_HD - ROT_DIM), F32)
    zero = jnp.zeros((n, DIFF_HD - ROT_DIM), F32)
    zh = jnp.zeros((n, half), F32)
    c = jnp.concatenate([cos, cos, one], axis=1)
    s1 = jnp.concatenate([-sin, zh, zero], axis=1)
    s2 = jnp.concatenate([zh, sin, zero], axis=1)
    return tuple(jnp.concatenate([t, t], axis=1) for t in (c, s1, s2))


def _gla_body(x_ref, g_ref, w_ref, wa_ref, ba_ref, s0_ref, o_ref, s_ref, st_ref, *, chunk):
    t = pl.program_id(1)
    C = chunk

    @pl.when(t == 0)
    def _():
        for h in range(GLA_HEADS):
            st_ref[h] = s0_ref[0, h].T

    xn = _rms_rows(x_ref[0], g_ref[...]).astype(BF16)
    u = jnp.dot(xn, w_ref[...], preferred_element_type=F32)
    z = jnp.dot(u[:, 2 * GLA_KEY + GLA_VAL:].astype(BF16), wa_ref[...],
                preferred_element_type=F32) + ba_ref[...]
    log_a = (jnp.minimum(z, 0.0) - jnp.log1p(jnp.exp(-jnp.abs(z)))) * (1.0 / GLA_TAU)

    ri = lax.broadcasted_iota(jnp.int32, (C, C), 0)
    ci = lax.broadcasted_iota(jnp.int32, (C, C), 1)
    tril = ri >= ci
    tril_b = jnp.where(tril, 1.0, 0.0).astype(BF16)

    for c in range(x_ref.shape[1] // C):
        rows = slice(c * C, (c + 1) * C)
        la = log_a[rows]
        hi = la.astype(BF16)
        lo = (la - hi.astype(F32)).astype(BF16)
        b = (jnp.dot(tril_b, hi, preferred_element_type=F32)
             + jnp.dot(tril_b, lo, preferred_element_type=F32))
        r = b[C // 2 - 1:C // 2]
        bl = b[C - 1:C]
        e1 = jnp.exp(b - r)
        e2 = jnp.exp(r - b)
        q1 = u[rows, 0:GLA_KEY] * GLA_SCALE * e1
        qs = (q1 * jnp.exp(r)).astype(BF16)
        q1 = q1.astype(BF16)
        k1 = u[rows, GLA_KEY:2 * GLA_KEY] * e2
        kd = (k1 * jnp.exp(bl - r)).astype(BF16)
        k1 = k1.astype(BF16)
        dec = jnp.exp(bl)
        v = u[rows, 2 * GLA_KEY:2 * GLA_KEY + GLA_VAL].astype(BF16)
        for h in range(GLA_HEADS):
            ks = slice(h * GLA_DK, (h + 1) * GLA_DK)
            vs = slice(h * GLA_DV, (h + 1) * GLA_DV)
            a = lax.dot_general(q1[:, ks], k1[:, ks], _NT, preferred_element_type=F32)
            a = jnp.where(tril, a, 0.0).astype(BF16)
            st = st_ref[h]
            o = (jnp.dot(a, v[:, vs], preferred_element_type=F32)
                 + lax.dot_general(qs[:, ks], st.astype(BF16), _NT, preferred_element_type=F32))
            o_ref[0, rows, vs] = o
            st_ref[h] = st * dec[:, ks] + lax.dot_general(v[:, vs], kd[:, ks], _TN,
                                                          preferred_element_type=F32)

    @pl.when(t == pl.num_programs(1) - 1)
    def _():
        for h in range(GLA_HEADS):
            s_ref[0, h] = st_ref[h].T


def _gla_call(x, g, w, wa, ba, s0, layer, tt, chunk):
    B, T, _ = x.shape
    row = pl.BlockSpec((1, tt, D_MODEL), lambda b, i: (b, i, 0))
    state_shape = (1, GLA_HEADS, GLA_DK, GLA_DV)
    return pl.pallas_call(
        functools.partial(_gla_body, chunk=chunk),
        grid=(B, T // tt),
        in_specs=[row, _const_spec((1, D_MODEL)), _const_spec(w.shape), _const_spec(wa.shape),
                  _const_spec((1, GLA_KEY)), pl.BlockSpec(state_shape, lambda b, i: (layer * B + b, 0, 0, 0))],
        out_specs=[row, pl.BlockSpec(state_shape, lambda b, i: (b, 0, 0, 0))],
        out_shape=[jax.ShapeDtypeStruct((B, T, GLA_VAL), F32),
                   jax.ShapeDtypeStruct((B, GLA_HEADS, GLA_DK, GLA_DV), F32)],
        scratch_shapes=[pltpu.VMEM((GLA_HEADS, GLA_DV, GLA_DK), F32)],
        compiler_params=_params(2),
        name="gla",
    )(x, g, w, wa, ba, s0)


def _lam(lq1, lk1, lq2, lk2, lambda_init):
    return (jnp.exp(jnp.sum(lq1[...] * lk1[...], axis=-1, keepdims=True))
            - jnp.exp(jnp.sum(lq2[...] * lk2[...], axis=-1, keepdims=True)) + lambda_init)


def _split_maps(q):
    lane = lax.broadcasted_iota(jnp.int32, q.shape, 1)
    zero = jnp.zeros_like(q)
    return jnp.where(lane < DIFF_HD, q, zero), jnp.where(lane >= DIFF_HD, q, zero)


def _ones_column(rows):
    lane = lax.broadcasted_iota(jnp.int32, (rows, LANES), 1)
    return jnp.where(lane == 0, 1.0, 0.0).astype(BF16)


def _attn_body(lq1, lk1, lq2, lk2, q_ref, k_ref, v_ref, bias_ref, *rest, n, tq, lambda_init):
    o_ref, s_ref = rest[-2:]
    nblk = tq // LANES
    qz = _split_maps(q_ref[0, 0])
    mx = [None, None]
    for j in range(n):
        kt = k_ref[0, 0, j * tq:(j + 1) * tq, :]
        for m in range(2):
            s = lax.dot_general(qz[m], kt, _NT, preferred_element_type=F32)
            if j == n - 1:
                s = s + bias_ref[...]
            s_ref[m, j] = s
            fold = functools.reduce(jnp.maximum, [s[:, c * LANES:(c + 1) * LANES] for c in range(nblk)])
            mx[m] = fold if mx[m] is None else jnp.maximum(mx[m], fold)
    ones = _ones_column(tq)
    acc = []
    for m in range(2):
        mrow = jnp.broadcast_to(jnp.max(mx[m], axis=-1, keepdims=True), (tq, LANES))
        mrow = jnp.concatenate([mrow] * nblk, axis=1)
        a = None
        for j in range(n):
            vt = jnp.concatenate([v_ref[0, 0, j * tq:(j + 1) * tq, :], ones], axis=1)
            p = jnp.exp2(s_ref[m, j] - mrow).astype(BF16)
            d = jnp.dot(p, vt, preferred_element_type=F32)
            a = d if a is None else a + d
        acc.append(a)
    lam = _lam(lq1, lk1, lq2, lk2, lambda_init)
    o_ref[0] = (acc[0][:, 0:DIFF_VD] / acc[0][:, DIFF_VD:DIFF_VD + 1]
                - lam * (acc[1][:, 0:DIFF_VD] / acc[1][:, DIFF_VD:DIFF_VD + 1]))


def _attn_call(lams, q, k, v, bias, o_prev, n, tq, lambda_init):
    B, _, T, _ = q.shape
    lam_spec = _const_spec((1, DIFF_HD))
    qs = pl.BlockSpec((1, 1, tq, LANES), lambda b, h: (b, h, n - 1, 0))
    kv = pl.BlockSpec((1, 1, n * tq, LANES), lambda b, h: (b, h, 0, 0))
    in_specs = [lam_spec] * 4 + [qs, kv, kv, _const_spec((tq, tq))]
    args = (*lams, q, k, v, bias)
    aliases = {}
    if o_prev is not None:
        in_specs.append(pl.BlockSpec(memory_space=pl.ANY))
        aliases = {len(args): 0}
        args = args + (o_prev,)
    return pl.pallas_call(
        functools.partial(_attn_body, n=n, tq=tq, lambda_init=lambda_init),
        grid=(B, DIFF_HEADS),
        in_specs=in_specs,
        out_specs=pl.BlockSpec((1, tq, LANES), lambda b, h: (b, n - 1, h)),
        out_shape=jax.ShapeDtypeStruct((B, T, DIFF_VAL), F32),
        scratch_shapes=[pltpu.VMEM((2, n, tq, tq), F32)],
        input_output_aliases=aliases,
        compiler_params=_params(2),
        name=f"diff_attn_q{n - 1}",
    )(*args)


def _diag_bias(tq):
    r = jnp.arange(tq, dtype=jnp.int32) // CHUNK
    return jnp.where(r[:, None] >= r[None, :], 0.0, -jnp.inf).astype(F32)


def _attn_cache_body(lq1, lk1, lq2, lk2, q_ref, kc_ref, vc_ref, kn_ref, vn_ref, o_ref, *, lambda_init):
    T = q_ref.shape[2]
    P = kc_ref.shape[1] // DIFF_HEADS
    lam = _lam(lq1, lk1, lq2, lk2, lambda_init)
    for h in range(DIFF_HEADS):
        qz = jnp.concatenate(_split_maps(q_ref[0, h]), axis=0)
        kc = kc_ref[0, pl.ds(h, P, stride=DIFF_HEADS), :].astype(BF16)
        vc = vc_ref[0, pl.ds(h, P, stride=DIFF_HEADS), :].astype(BF16)
        sc = lax.dot_general(qz, kc, _NT, preferred_element_type=F32)
        sn = lax.dot_general(qz, kn_ref[0, h], _NT, preferred_element_type=F32)
        mx = jnp.maximum(jnp.max(sc, axis=-1, keepdims=True), jnp.max(sn, axis=-1, keepdims=True))
        pc = jnp.exp2(sc - mx)
        pn = jnp.exp2(sn - mx)
        l = jnp.sum(pc, axis=-1, keepdims=True) + jnp.sum(pn, axis=-1, keepdims=True)
        o = (jnp.dot(pc.astype(BF16), vc, preferred_element_type=F32)
             + jnp.dot(pn.astype(BF16), vn_ref[0, h], preferred_element_type=F32)) / l
        o_ref[0, :, h * DIFF_VD:(h + 1) * DIFF_VD] = o[0:T] - lam * o[T:2 * T]


def _attn_cache_call(lams, q, kc, vc, kn, vn, layer, lambda_init):
    B, _, T, _ = q.shape
    lam_spec = _const_spec((1, DIFF_HD))
    new = pl.BlockSpec((1, DIFF_HEADS, T, LANES), lambda b: (b, 0, 0, 0))
    past = pl.BlockSpec((1, kc.shape[1], LANES), lambda b: (layer * B + b, 0, 0))
    return pl.pallas_call(
        functools.partial(_attn_cache_body, lambda_init=lambda_init),
        grid=(B,),
        in_specs=[lam_spec] * 4 + [new, past, past, new, new],
        out_specs=pl.BlockSpec((1, T, DIFF_VAL), lambda b: (b, 0, 0)),
        out_shape=jax.ShapeDtypeStruct((B, T, DIFF_VAL), F32),
        compiler_params=_params(1),
        name="diff_attn_cache",
    )(*lams, q, kc, vc, kn, vn)


def _merge_body(x_ref, og_ref, od_ref, gpre_ref, wg_ref, ggla_ref, gsub_ref, wpg_ref, wpd_ref, wo_ref,
                gpost_ref, y_ref, yg_s, yd_s, *, lambda_init):
    x = x_ref[...]
    xn = _rms_rows(x, gpre_ref[...]).astype(BF16)
    gates = jnp.dot(xn, wg_ref[...], preferred_element_type=F32)
    og, od = og_ref[...], od_ref[...]
    for h in range(GLA_HEADS):
        sl = slice(h * GLA_DV, (h + 1) * GLA_DV)
        yg_s[:, sl] = (_rms_rows(og[:, sl], ggla_ref[...]) * jax.nn.silu(gates[:, sl])).astype(BF16)
    for h in range(DIFF_HEADS):
        sl = slice(h * DIFF_VD, (h + 1) * DIFF_VD)
        dg = gates[:, GLA_VAL + h * DIFF_VD:GLA_VAL + (h + 1) * DIFF_VD]
        yd_s[:, sl] = (_rms_rows(od[:, sl], gsub_ref[...]) * (1.0 - lambda_init) * jax.nn.silu(dg)).astype(BF16)
    ma = gates[:, GLA_VAL + DIFF_VAL:GLA_VAL + DIFF_VAL + D_MODEL]
    mb = gates[:, GLA_VAL + DIFF_VAL + D_MODEL:]
    merged = (jax.nn.sigmoid(ma) * jnp.dot(yg_s[...], wpg_ref[...], preferred_element_type=F32)
              + jax.nn.sigmoid(mb) * jnp.dot(yd_s[...], wpd_ref[...], preferred_element_type=F32))
    z = jnp.dot(merged.astype(BF16), wo_ref[...], preferred_element_type=F32)
    y_ref[...] = x + _rms_rows(z, gpost_ref[...])


def _merge_call(x, og, od, gpre, wg, ggla, gsub, wpg, wpd, wo, gpost, tm, lambda_init):
    N = x.shape[0]
    row = pl.BlockSpec((tm, D_MODEL), lambda i: (i, 0))
    return pl.pallas_call(
        functools.partial(_merge_body, lambda_init=lambda_init),
        grid=(N // tm,),
        in_specs=[row, row, row, _const_spec((1, D_MODEL)), _const_spec(wg.shape), _const_spec((1, GLA_DV)),
                  _const_spec((1, DIFF_VD)), _const_spec(wpg.shape), _const_spec(wpd.shape),
                  _const_spec(wo.shape), _const_spec((1, D_MODEL))],
        out_specs=row,
        out_shape=jax.ShapeDtypeStruct((N, D_MODEL), F32),
        scratch_shapes=[pltpu.VMEM((tm, GLA_VAL), BF16), pltpu.VMEM((tm, DIFF_VAL), BF16)],
        compiler_params=_params(1),
        name="merge",
    )(x, og, od, gpre, wg, ggla, gsub, wpg, wpd, wo, gpost)


def _tile(n, target):
    t = min(n, target)
    while n % t:
        t -= 1
    return t


def kernel(x_prompt, x_sample, cache_diff_k, cache_diff_v, state_gla, pre_norm_g, w_in, gla_w_a2, gla_b_a,
           gla_norm_g, diff_lambda_q1, diff_lambda_k1, diff_lambda_q2, diff_lambda_k2, diff_subln_g,
           w_proj_gla, w_proj_diff, w_out, post_norm_g):
    depth = w_in.shape[0]
    past_len = cache_diff_k.shape[2]
    hp, hs = x_prompt, x_sample
    B, T, _ = hp.shape
    Bs, Ts, _ = hs.shape
    cache_k = cache_diff_k.reshape(depth * Bs, past_len * DIFF_HEADS, 2 * DIFF_HD)
    cache_v = cache_diff_v.reshape(depth * Bs, past_len * DIFF_HEADS, DIFF_VD)
    state_in = state_gla.astype(F32).reshape(depth * Bs, GLA_HEADS, GLA_DK, GLA_DV)
    zero_state = jnp.zeros((B, GLA_HEADS, GLA_DK, GLA_DV), F32)
    outs = [[] for _ in range(6)]
    offs = [0]
    for s in IN_SIZES:
        offs.append(offs[-1] + s)
    for l in range(depth):
        lambda_init = 0.8 - 0.6 * math.exp(-0.3 * l)
        w = w_in[l]
        seg = [w[:, offs[i]:offs[i + 1]].astype(BF16) for i in range(len(IN_SIZES))]
        gq, gk, gv, gr, gg, dq, dk, dv, dg, ma, mb = seg
        w_gla = jnp.concatenate([gq, gk, gv, jnp.pad(gr, ((0, 0), (0, LANES - GLA_RANK)))], axis=1)
        w_qkv = jnp.concatenate([dq, dk, dv], axis=1)
        w_gate = jnp.concatenate([gg, dg, ma, mb], axis=1)
        wa = jnp.pad(gla_w_a2[l], ((0, LANES - GLA_RANK), (0, 0))).astype(BF16)
        ba = gla_b_a[l][None]
        gpre = pre_norm_g[l][None]
        lams = (diff_lambda_q1[l][None], diff_lambda_k1[l][None], diff_lambda_q2[l][None], diff_lambda_k2[l][None])
        merge_w = (gpre, w_gate, gla_norm_g[l][None], diff_subln_g[l][None], w_proj_gla[l].astype(BF16),
                   w_proj_diff[l].astype(BF16), w_out[l].astype(BF16), post_norm_g[l][None])

        k, v, q16, k16, v16 = _qkv_call(hp, gpre, w_qkv, _rope_tables(jnp.arange(T, dtype=jnp.int32)),
                                        _tile(T, 512))
        o_gla, s_p = _gla_call(hp, gpre, w_gla, wa, ba, zero_state, 0, _tile(T, 256), _tile(T, 64))
        tq = _tile(T, 512)
        bias = _diag_bias(tq)
        o_diff = None
        for n in range(1, T // tq + 1):
            o_diff = _attn_call(lams, q16, k16, v16, bias, o_diff, n, tq, lambda_init)
        hp = _merge_call(hp.reshape(B * T, D_MODEL), o_gla.reshape(B * T, GLA_VAL),
                         o_diff.reshape(B * T, DIFF_VAL), *merge_w, _tile(B * T, 256),
                         lambda_init).reshape(B, T, D_MODEL)
        outs[0].append(k.reshape(B, T, DIFF_HEADS, DIFF_VD))
        outs[1].append(v.reshape(B, T, DIFF_HEADS, DIFF_VD))
        outs[2].append(s_p)

        pos_s = past_len + jnp.arange(Ts, dtype=jnp.int32)
        k, v, q16, k16, v16 = _qkv_call(hs, gpre, w_qkv, _rope_tables(pos_s), Ts)
        o_gla, s_s = _gla_call(hs, gpre, w_gla, wa, ba, state_in, l, Ts, Ts)
        o_diff = _attn_cache_call(lams, q16, cache_k, cache_v, k16, v16, l, lambda_init)
        hs = _merge_call(hs.reshape(Bs * Ts, D_MODEL), o_gla.reshape(Bs * Ts, GLA_VAL),
                         o_diff.reshape(Bs * Ts, DIFF_VAL), *merge_w, _tile(Bs * Ts, 256),
                         lambda_init).reshape(Bs, Ts, D_MODEL)
        outs[3].append(k.reshape(Bs, Ts, DIFF_HEADS, DIFF_VD))
        outs[4].append(v.reshape(Bs, Ts, DIFF_HEADS, DIFF_VD))
        outs[5].append(s_s)

    return (hp, hs, jnp.stack(outs[0]), jnp.stack(outs[1]), jnp.stack(outs[2]),
            jnp.stack(outs[3]), jnp.stack(outs[4]), jnp.stack(outs[5]))
```

```python
import functools
import math

import jax
import jax.numpy as jnp
from jax import lax
from jax.experimental import pallas as pl
from jax.experimental.pallas import tpu as pltpu

F32 = jnp.float32
BF16 = jnp.bfloat16

D_MODEL = 1024
EPS = 1e-6
CHUNK = 64

GLA_HEADS = 4
GLA_DK = 128
GLA_DV = 256
GLA_KEY = GLA_HEADS * GLA_DK
GLA_VAL = GLA_HEADS * GLA_DV
GLA_RANK = 16
GLA_TAU = 16.0
GLA_SCALE = GLA_DK ** -0.5

DIFF_HEADS = 8
DIFF_HD = 64
DIFF_VD = 128
DIFF_QK = DIFF_HEADS * 2 * DIFF_HD
DIFF_VAL = DIFF_HEADS * DIFF_VD
ROT_DIM = DIFF_HD // 4
ROPE_THETA = 500000.0
QK_SCALE = DIFF_HD ** -0.5
Q_PRESCALE = QK_SCALE * math.log2(math.e)

IN_SIZES = (GLA_KEY, GLA_KEY, GLA_VAL, GLA_RANK, GLA_VAL,
            DIFF_QK, DIFF_QK, DIFF_VAL, DIFF_VAL, D_MODEL, D_MODEL)

LANES = 128
VMEM_LIMIT = 56 * 1024 * 1024

_NT = (((1,), (1,)), ((), ()))
_TN = (((0,), (0,)), ((), ()))


def _rms_rows(x, g):
    return x * lax.rsqrt(jnp.mean(x * x, axis=-1, keepdims=True) + EPS) * g


def _params(n_axes):
    return pltpu.CompilerParams(dimension_semantics=("arbitrary",) * n_axes,
                                vmem_limit_bytes=VMEM_LIMIT)


def _const_spec(shape):
    zeros = (0,) * len(shape)
    return pl.BlockSpec(shape, lambda *_: zeros)


def _qkv_body(x_ref, g_ref, w_ref, c_ref, s1_ref, s2_ref, k_ref, v_ref, q16_ref, k16_ref, v16_ref):
    xn = _rms_rows(x_ref[0], g_ref[...]).astype(BF16)
    c, s1, s2 = c_ref[...], s1_ref[...], s2_ref[...]

    def rope(t):
        return t * c + pltpu.roll(t, LANES - ROT_DIM // 2, 1) * s1 + pltpu.roll(t, ROT_DIM // 2, 1) * s2

    q = jnp.dot(xn, w_ref[:, 0:DIFF_QK], preferred_element_type=F32)
    for h in range(DIFF_HEADS):
        sl = slice(h * LANES, (h + 1) * LANES)
        q16_ref[0, h] = (rope(q[:, sl]) * Q_PRESCALE).astype(BF16)
    k = jnp.dot(xn, w_ref[:, DIFF_QK:2 * DIFF_QK], preferred_element_type=F32)
    for h in range(DIFF_HEADS):
        sl = slice(h * LANES, (h + 1) * LANES)
        kr = rope(k[:, sl])
        k_ref[0, :, sl] = kr
        k16_ref[0, h] = kr.astype(BF16)
    v = jnp.dot(xn, w_ref[:, 2 * DIFF_QK:], preferred_element_type=F32)
    v_ref[0] = v
    for h in range(DIFF_HEADS):
        v16_ref[0, h] = v[:, h * LANES:(h + 1) * LANES].astype(BF16)


def _qkv_call(x, g, w, tabs, tm):
    B, T, _ = x.shape
    row = pl.BlockSpec((1, tm, D_MODEL), lambda b, i: (b, i, 0))
    heads = pl.BlockSpec((1, DIFF_HEADS, tm, LANES), lambda b, i: (b, 0, i, 0))
    tab = pl.BlockSpec((tm, LANES), lambda b, i: (i, 0))
    head_major = jax.ShapeDtypeStruct((B, DIFF_HEADS, T, LANES), BF16)
    return pl.pallas_call(
        _qkv_body,
        grid=(B, T // tm),
        in_specs=[row, _const_spec((1, D_MODEL)), _const_spec(w.shape), tab, tab, tab],
        out_specs=[row, row, heads, heads, heads],
        out_shape=[jax.ShapeDtypeStruct((B, T, DIFF_QK), F32),
                   jax.ShapeDtypeStruct((B, T, DIFF_VAL), F32),
                   head_major, head_major, head_major],
        compiler_params=_params(2),
        name="qkv_proj",
    )(x, g, w, *tabs)


def _rope_tables(pos):
    half = ROT_DIM // 2
    inv = ROPE_THETA ** (-jnp.arange(0, ROT_DIM, 2, dtype=F32) / ROT_DIM)
    ang = pos.astype(F32)[:, None] * inv[None, :]
    cos, sin = jnp.cos(ang), jnp.sin(ang)
    n = pos.shape[0]
    one = jnp.ones((n, DIFF_HD - ROT_DIM), F32)
    zero = jnp.zeros((n, DIFF_HD - ROT_DIM), F32)
    zh = jnp.zeros((n, half), F32)
    c = jnp.concatenate([cos, cos, one], axis=1)
    s1 = jnp.concatenate([-sin, zh, zero], axis=1)
    s2 = jnp.concatenate([zh, sin, zero], axis=1)
    return tuple(jnp.concatenate([t, t], axis=1) for t in (c, s1, s2))


def _gla_body(x_ref, g_ref, w_ref, wa_ref, ba_ref, s0_ref, o_ref, s_ref, st_ref, *, chunk):
    t = pl.program_id(1)
    C = chunk

    @pl.when(t == 0)
    def _():
        for h in range(GLA_HEADS):
            st_ref[h] = s0_ref[0, h].T

    ri = lax.broadcasted_iota(jnp.int32, (C, C), 0)
    ci = lax.broadcasted_iota(jnp.int32, (C, C), 1)
    tril = ri >= ci
    tril_b = jnp.where(tril, 1.0, 0.0).astype(BF16)

    heads = range(GLA_HEADS)
    ks = [slice(h * GLA_DK, (h + 1) * GLA_DK) for h in heads]
    vs = [slice(h * GLA_DV, (h + 1) * GLA_DV) for h in heads]

    n_chunks = x_ref.shape[1] // C
    xn = _rms_rows(x_ref[0], g_ref[...]).astype(BF16)
    u_all = jnp.dot(xn, w_ref[...], preferred_element_type=F32)
    z = jnp.dot(u_all[:, 2 * GLA_KEY + GLA_VAL:].astype(BF16), wa_ref[...],
                preferred_element_type=F32) + ba_ref[...]
    la_all = (jnp.minimum(z, 0.0) - jnp.log(1.0 + jnp.exp(-jnp.abs(z)))) * (1.0 / GLA_TAU)

    def operands(c):
        u, la = u_all[c * C:(c + 1) * C], la_all[c * C:(c + 1) * C]
        hi = la.astype(BF16)
        lo = (la - hi.astype(F32)).astype(BF16)
        b = (jnp.dot(tril_b, hi, preferred_element_type=F32)
             + jnp.dot(tril_b, lo, preferred_element_type=F32))
        r = b[C // 2 - 1:C // 2]
        bl = b[C - 1:C]
        q1 = u[:, 0:GLA_KEY] * GLA_SCALE * jnp.exp(b - r)
        qs = (q1 * jnp.exp(r)).astype(BF16)
        k1 = u[:, GLA_KEY:2 * GLA_KEY] * jnp.exp(r - b)
        kd = (k1 * jnp.exp(bl - r)).astype(BF16)
        v = u[:, 2 * GLA_KEY:2 * GLA_KEY + GLA_VAL].astype(BF16)
        return q1.astype(BF16), qs, k1.astype(BF16), kd, v, jnp.exp(bl)

    nxt = operands(0)
    for c in range(n_chunks):
        q1, qs, k1, kd, v, dec = nxt
        rows = slice(c * C, (c + 1) * C)
        a_raw = [lax.dot_general(q1[:, ks[h]], k1[:, ks[h]], _NT, preferred_element_type=F32) for h in heads]
        st = [st_ref[h] for h in heads]
        o_inter = [lax.dot_general(qs[:, ks[h]], st[h].astype(BF16), _NT, preferred_element_type=F32)
                   for h in heads]
        kv = [lax.dot_general(v[:, vs[h]], kd[:, ks[h]], _TN, preferred_element_type=F32) for h in heads]
        if c + 1 < n_chunks:
            nxt = operands(c + 1)
        for h in heads:
            a = jnp.where(tril, a_raw[h], 0.0).astype(BF16)
            o_ref[0, rows, vs[h]] = jnp.dot(a, v[:, vs[h]], preferred_element_type=F32) + o_inter[h]
            st_ref[h] = st[h] * dec[:, ks[h]] + kv[h]

    @pl.when(t == pl.num_programs(1) - 1)
    def _():
        for h in range(GLA_HEADS):
            s_ref[0, h] = st_ref[h].T


def _gla_call(x, g, w, wa, ba, s0, layer, tt, chunk):
    B, T, _ = x.shape
    row = pl.BlockSpec((1, tt, D_MODEL), lambda b, i: (b, i, 0))
    state_shape = (1, GLA_HEADS, GLA_DK, GLA_DV)
    return pl.pallas_call(
        functools.partial(_gla_body, chunk=chunk),
        grid=(B, T // tt),
        in_specs=[row, _const_spec((1, D_MODEL)), _const_spec(w.shape), _const_spec(wa.shape),
                  _const_spec((1, GLA_KEY)), pl.BlockSpec(state_shape, lambda b, i: (layer * B + b, 0, 0, 0))],
        out_specs=[row, pl.BlockSpec(state_shape, lambda b, i: (b, 0, 0, 0))],
        out_shape=[jax.ShapeDtypeStruct((B, T, GLA_VAL), F32),
                   jax.ShapeDtypeStruct((B, GLA_HEADS, GLA_DK, GLA_DV), F32)],
        scratch_shapes=[pltpu.VMEM((GLA_HEADS, GLA_DV, GLA_DK), F32)],
        compiler_params=_params(2),
        name="gla",
    )(x, g, w, wa, ba, s0)


def _lam(lq1, lk1, lq2, lk2, lambda_init):
    return (jnp.exp(jnp.sum(lq1[...] * lk1[...], axis=-1, keepdims=True))
            - jnp.exp(jnp.sum(lq2[...] * lk2[...], axis=-1, keepdims=True)) + lambda_init)


def _split_maps(q):
    lane = lax.broadcasted_iota(jnp.int32, q.shape, 1)
    zero = jnp.zeros_like(q)
    return jnp.where(lane < DIFF_HD, q, zero), jnp.where(lane >= DIFF_HD, q, zero)


def _ones_column(rows):
    lane = lax.broadcasted_iota(jnp.int32, (rows, LANES), 1)
    return jnp.where(lane == 0, 1.0, 0.0).astype(BF16)


def _attn_body(lq1, lk1, lq2, lk2, q_ref, k_ref, v_ref, bias_ref, *rest, n, tq, lambda_init):
    o_ref, s_ref = rest[-2:]
    nblk = tq // LANES
    hq = tq // 2
    qz = _split_maps(q_ref[0, 0])

    def lane_fold(s):
        return functools.reduce(jnp.maximum, [s[:, c * LANES:(c + 1) * LANES] for c in range(s.shape[1] // LANES)])

    mx = [None, None]
    for j in range(n):
        kt = k_ref[0, 0, j * tq:(j + 1) * tq, :]
        for m in range(2):
            if j < n - 1:
                s = lax.dot_general(qz[m], kt, _NT, preferred_element_type=F32)
                s_ref[m, j] = s
                fold = lane_fold(s)
            else:
                top = lax.dot_general(qz[m][:hq], kt[:hq], _NT, preferred_element_type=F32) + bias_ref[:hq, :hq]
                bot = lax.dot_general(qz[m][hq:], kt, _NT, preferred_element_type=F32) + bias_ref[hq:, :]
                s_ref[m, j, :hq, :hq] = top
                s_ref[m, j, hq:, :] = bot
                fold = jnp.concatenate([lane_fold(top), lane_fold(bot)], axis=0)
            mx[m] = fold if mx[m] is None else jnp.maximum(mx[m], fold)
    ones = _ones_column(tq)
    acc = []
    for m in range(2):
        mrow = jnp.broadcast_to(jnp.max(mx[m], axis=-1, keepdims=True), (tq, LANES))
        mrow = jnp.concatenate([mrow] * nblk, axis=1)
        a = None
        for j in range(n):
            vt = jnp.concatenate([v_ref[0, 0, j * tq:(j + 1) * tq, :], ones], axis=1)
            if j < n - 1:
                p = jnp.exp2(s_ref[m, j] - mrow).astype(BF16)
                d = jnp.dot(p, vt, preferred_element_type=F32)
            else:
                p_top = jnp.exp2(s_ref[m, j, :hq, :hq] - mrow[:hq, :hq]).astype(BF16)
                p_bot = jnp.exp2(s_ref[m, j, hq:, :] - mrow[hq:]).astype(BF16)
                d = jnp.concatenate([jnp.dot(p_top, vt[:hq], preferred_element_type=F32),
                                     jnp.dot(p_bot, vt, preferred_element_type=F32)], axis=0)
            a = d if a is None else a + d
        acc.append(a)
    lam = _lam(lq1, lk1, lq2, lk2, lambda_init)
    o_ref[0] = (acc[0][:, 0:DIFF_VD] / acc[0][:, DIFF_VD:DIFF_VD + 1]
                - lam * (acc[1][:, 0:DIFF_VD] / acc[1][:, DIFF_VD:DIFF_VD + 1]))


def _attn_call(lams, q, k, v, bias, o_prev, n, tq, lambda_init):
    B, _, T, _ = q.shape
    assert tq % (2 * LANES) == 0 and (tq // 2) % CHUNK == 0, tq
    lam_spec = _const_spec((1, DIFF_HD))
    qs = pl.BlockSpec((1, 1, tq, LANES), lambda b, h: (b, h, n - 1, 0))
    kv = pl.BlockSpec((1, 1, n * tq, LANES), lambda b, h: (b, h, 0, 0))
    in_specs = [lam_spec] * 4 + [qs, kv, kv, _const_spec((tq, tq))]
    args = (*lams, q, k, v, bias)
    aliases = {}
    if o_prev is not None:
        in_specs.append(pl.BlockSpec(memory_space=pl.ANY))
        aliases = {len(args): 0}
        args = args + (o_prev,)
    return pl.pallas_call(
        functools.partial(_attn_body, n=n, tq=tq, lambda_init=lambda_init),
        grid=(B, DIFF_HEADS),
        in_specs=in_specs,
        out_specs=pl.BlockSpec((1, tq, LANES), lambda b, h: (b, n - 1, h)),
        out_shape=jax.ShapeDtypeStruct((B, T, DIFF_VAL), F32),
        scratch_shapes=[pltpu.VMEM((2, n, tq, tq), F32)],
        input_output_aliases=aliases,
        compiler_params=_params(2),
        name=f"diff_attn_q{n - 1}",
    )(*args)


def _diag_bias(tq):
    r = jnp.arange(tq, dtype=jnp.int32) // CHUNK
    return jnp.where(r[:, None] >= r[None, :], 0.0, -jnp.inf).astype(F32)


def _attn_cache_body(lq1, lk1, lq2, lk2, q_ref, kc_ref, vc_ref, kn_ref, vn_ref, o_ref, *, lambda_init):
    T = q_ref.shape[2]
    P = kc_ref.shape[1] // DIFF_HEADS
    lam = _lam(lq1, lk1, lq2, lk2, lambda_init)
    for h in range(DIFF_HEADS):
        qz = jnp.concatenate(_split_maps(q_ref[0, h]), axis=0)
        kc = kc_ref[0, pl.ds(h, P, stride=DIFF_HEADS), :].astype(BF16)
        vc = vc_ref[0, pl.ds(h, P, stride=DIFF_HEADS), :].astype(BF16)
        sc = lax.dot_general(qz, kc, _NT, preferred_element_type=F32)
        sn = lax.dot_general(qz, kn_ref[0, h], _NT, preferred_element_type=F32)
        mx = jnp.maximum(jnp.max(sc, axis=-1, keepdims=True), jnp.max(sn, axis=-1, keepdims=True))
        pc = jnp.exp2(sc - mx)
        pn = jnp.exp2(sn - mx)
        l = jnp.sum(pc, axis=-1, keepdims=True) + jnp.sum(pn, axis=-1, keepdims=True)
        o = (jnp.dot(pc.astype(BF16), vc, preferred_element_type=F32)
             + jnp.dot(pn.astype(BF16), vn_ref[0, h], preferred_element_type=F32)) / l
        o_ref[0, :, h * DIFF_VD:(h + 1) * DIFF_VD] = o[0:T] - lam * o[T:2 * T]


def _attn_cache_call(lams, q, kc, vc, kn, vn, layer, lambda_init):
    B, _, T, _ = q.shape
    lam_spec = _const_spec((1, DIFF_HD))
    new = pl.BlockSpec((1, DIFF_HEADS, T, LANES), lambda b: (b, 0, 0, 0))
    past = pl.BlockSpec((1, kc.shape[1], LANES), lambda b: (layer * B + b, 0, 0))
    return pl.pallas_call(
        functools.partial(_attn_cache_body, lambda_init=lambda_init),
        grid=(B,),
        in_specs=[lam_spec] * 4 + [new, past, past, new, new],
        out_specs=pl.BlockSpec((1, T, DIFF_VAL), lambda b: (b, 0, 0)),
        out_shape=jax.ShapeDtypeStruct((B, T, DIFF_VAL), F32),
        compiler_params=_params(1),
        name="diff_attn_cache",
    )(*lams, q, kc, vc, kn, vn)


def _merge_body(x_ref, og_ref, od_ref, gpre_ref, wg_ref, ggla_ref, gsub_ref, wpg_ref, wpd_ref, wo_ref,
                gpost_ref, y_ref, yg_s, yd_s, *, lambda_init):
    x = x_ref[...]
    xn = _rms_rows(x, gpre_ref[...]).astype(BF16)
    gates = jnp.dot(xn, wg_ref[...], preferred_element_type=F32)
    og, od = og_ref[...], od_ref[...]
    for h in range(GLA_HEADS):
        sl = slice(h * GLA_DV, (h + 1) * GLA_DV)
        yg_s[:, sl] = (_rms_rows(og[:, sl], ggla_ref[...]) * jax.nn.silu(gates[:, sl])).astype(BF16)
    for h in range(DIFF_HEADS):
        sl = slice(h * DIFF_VD, (h + 1) * DIFF_VD)
        dg = gates[:, GLA_VAL + h * DIFF_VD:GLA_VAL + (h + 1) * DIFF_VD]
        yd_s[:, sl] = (_rms_rows(od[:, sl], gsub_ref[...]) * (1.0 - lambda_init) * jax.nn.silu(dg)).astype(BF16)
    ma = gates[:, GLA_VAL + DIFF_VAL:GLA_VAL + DIFF_VAL + D_MODEL]
    mb = gates[:, GLA_VAL + DIFF_VAL + D_MODEL:]
    merged = (jax.nn.sigmoid(ma) * jnp.dot(yg_s[...], wpg_ref[...], preferred_element_type=F32)
              + jax.nn.sigmoid(mb) * jnp.dot(yd_s[...], wpd_ref[...], preferred_element_type=F32))
    z = jnp.dot(merged.astype(BF16), wo_ref[...], preferred_element_type=F32)
    y_ref[...] = x + _rms_rows(z, gpost_ref[...])


def _merge_call(x, og, od, gpre, wg, ggla, gsub, wpg, wpd, wo, gpost, tm, lambda_init):
    N = x.shape[0]
    row = pl.BlockSpec((tm, D_MODEL), lambda i: (i, 0))
    return pl.pallas_call(
        functools.partial(_merge_body, lambda_init=lambda_init),
        grid=(N // tm,),
        in_specs=[row, row, row, _const_spec((1, D_MODEL)), _const_spec(wg.shape), _const_spec((1, GLA_DV)),
                  _const_spec((1, DIFF_VD)), _const_spec(wpg.shape), _const_spec(wpd.shape),
                  _const_spec(wo.shape), _const_spec((1, D_MODEL))],
        out_specs=row,
        out_shape=jax.ShapeDtypeStruct((N, D_MODEL), F32),
        scratch_shapes=[pltpu.VMEM((tm, GLA_VAL), BF16), pltpu.VMEM((tm, DIFF_VAL), BF16)],
        compiler_params=_params(1),
        name="merge",
    )(x, og, od, gpre, wg, ggla, gsub, wpg, wpd, wo, gpost)


def _tile(n, target):
    t = min(n, target)
    while n % t:
        t -= 1
    return t


def kernel(x_prompt, x_sample, cache_diff_k, cache_diff_v, state_gla, pre_norm_g, w_in, gla_w_a2, gla_b_a,
           gla_norm_g, diff_lambda_q1, diff_lambda_k1, diff_lambda_q2, diff_lambda_k2, diff_subln_g,
           w_proj_gla, w_proj_diff, w_out, post_norm_g):
    depth = w_in.shape[0]
    past_len = cache_diff_k.shape[2]
    hp, hs = x_prompt, x_sample
    B, T, _ = hp.shape
    Bs, Ts, _ = hs.shape
    cache_k = cache_diff_k.reshape(depth * Bs, past_len * DIFF_HEADS, 2 * DIFF_HD)
    cache_v = cache_diff_v.reshape(depth * Bs, past_len * DIFF_HEADS, DIFF_VD)
    state_in = state_gla.astype(F32).reshape(depth * Bs, GLA_HEADS, GLA_DK, GLA_DV)
    zero_state = jnp.zeros((B, GLA_HEADS, GLA_DK, GLA_DV), F32)
    outs = [[] for _ in range(6)]
    offs = [0]
    for s in IN_SIZES:
        offs.append(offs[-1] + s)
    for l in range(depth):
        lambda_init = 0.8 - 0.6 * math.exp(-0.3 * l)
        def cols(first, last):
            return w_in[l, :, offs[first]:offs[last + 1]].astype(BF16)

        w_gla = jnp.concatenate([cols(0, 2), jnp.pad(cols(3, 3), ((0, 0), (0, LANES - GLA_RANK)))], axis=1)
        w_qkv = cols(5, 7)
        w_gate = jnp.concatenate([cols(4, 4), cols(8, 10)], axis=1)
        wa = jnp.pad(gla_w_a2[l], ((0, LANES - GLA_RANK), (0, 0))).astype(BF16)
        ba = gla_b_a[l][None]
        gpre = pre_norm_g[l][None]
        lams = (diff_lambda_q1[l][None], diff_lambda_k1[l][None], diff_lambda_q2[l][None], diff_lambda_k2[l][None])
        merge_w = (gpre, w_gate, gla_norm_g[l][None], diff_subln_g[l][None], w_proj_gla[l].astype(BF16),
                   w_proj_diff[l].astype(BF16), w_out[l].astype(BF16), post_norm_g[l][None])

        k, v, q16, k16, v16 = _qkv_call(hp, gpre, w_qkv, _rope_tables(jnp.arange(T, dtype=jnp.int32)),
                                        _tile(T, 512))
        o_gla, s_p = _gla_call(hp, gpre, w_gla, wa, ba, zero_state, 0, _tile(T, 512), _tile(T, 128))
        tq = _tile(T, 512)
        bias = _diag_bias(tq)
        o_diff = None
        for n in range(1, T // tq + 1):
            o_diff = _attn_call(lams, q16, k16, v16, bias, o_diff, n, tq, lambda_init)
        hp = _merge_call(hp.reshape(B * T, D_MODEL), o_gla.reshape(B * T, GLA_VAL),
                         o_diff.reshape(B * T, DIFF_VAL), *merge_w, _tile(B * T, 256),
                         lambda_init).reshape(B, T, D_MODEL)
        outs[0].append(k.reshape(B, T, DIFF_HEADS, DIFF_VD))
        outs[1].append(v.reshape(B, T, DIFF_HEADS, DIFF_VD))
        outs[2].append(s_p)

        pos_s = past_len + jnp.arange(Ts, dtype=jnp.int32)
        k, v, q16, k16, v16 = _qkv_call(hs, gpre, w_qkv, _rope_tables(pos_s), Ts)
        o_gla, s_s = _gla_call(hs, gpre, w_gla, wa, ba, state_in, l, Ts, Ts)
        o_diff = _attn_cache_call(lams, q16, cache_k, cache_v, k16, v16, l, lambda_init)
        hs = _merge_call(hs.reshape(Bs * Ts, D_MODEL), o_gla.reshape(Bs * Ts, GLA_VAL),
                         o_diff.reshape(Bs * Ts, DIFF_VAL), *merge_w, _tile(Bs * Ts, 256),
                         lambda_init).reshape(Bs, Ts, D_MODEL)
        outs[3].append(k.reshape(Bs, Ts, DIFF_HEADS, DIFF_VD))
        outs[4].append(v.reshape(Bs, Ts, DIFF_HEADS, DIFF_VD))
        outs[5].append(s_s)

    return (hp, hs, jnp.stack(outs[0]), jnp.stack(outs[1]), jnp.stack(outs[2]),
            jnp.stack(outs[3]), jnp.stack(outs[4]), jnp.stack(outs[5]))
```

```python
import functools
import math

import jax
import jax.numpy as jnp
from jax import lax
from jax.experimental import pallas as pl
from jax.experimental.pallas import tpu as pltpu

F32 = jnp.float32
BF16 = jnp.bfloat16

D_MODEL = 1024
EPS = 1e-6
CHUNK = 64

GLA_HEADS = 4
GLA_DK = 128
GLA_DV = 256
GLA_KEY = GLA_HEADS * GLA_DK
GLA_VAL = GLA_HEADS * GLA_DV
GLA_RANK = 16
GLA_TAU = 16.0
GLA_SCALE = GLA_DK ** -0.5

DIFF_HEADS = 8
DIFF_HD = 64
DIFF_VD = 128
DIFF_QK = DIFF_HEADS * 2 * DIFF_HD
DIFF_VAL = DIFF_HEADS * DIFF_VD
ROT_DIM = DIFF_HD // 4
ROPE_THETA = 500000.0
QK_SCALE = DIFF_HD ** -0.5
Q_PRESCALE = QK_SCALE * math.log2(math.e)

IN_SIZES = (GLA_KEY, GLA_KEY, GLA_VAL, GLA_RANK, GLA_VAL,
            DIFF_QK, DIFF_QK, DIFF_VAL, DIFF_VAL, D_MODEL, D_MODEL)

LANES = 128
VMEM_LIMIT = 56 * 1024 * 1024

_NT = (((1,), (1,)), ((), ()))
_TN = (((0,), (0,)), ((), ()))


def _rms_rows(x, g):
    return x * lax.rsqrt(jnp.mean(x * x, axis=-1, keepdims=True) + EPS) * g


def _params(n_axes):
    return pltpu.CompilerParams(dimension_semantics=("arbitrary",) * n_axes,
                                vmem_limit_bytes=VMEM_LIMIT)


def _const_spec(shape):
    zeros = (0,) * len(shape)
    return pl.BlockSpec(shape, lambda *_: zeros, pipeline_mode=pl.Buffered(1))


def _qkv_body(x_ref, g_ref, w_ref, c_ref, s1_ref, s2_ref, k_ref, v_ref, q16_ref, k16_ref, v16_ref):
    xn = _rms_rows(x_ref[0], g_ref[...]).astype(BF16)
    c, s1, s2 = c_ref[...], s1_ref[...], s2_ref[...]

    def rope(t):
        return t * c + pltpu.roll(t, LANES - ROT_DIM // 2, 1) * s1 + pltpu.roll(t, ROT_DIM // 2, 1) * s2

    q = jnp.dot(xn, w_ref[:, 0:DIFF_QK], preferred_element_type=F32)
    for h in range(DIFF_HEADS):
        sl = slice(h * LANES, (h + 1) * LANES)
        q16_ref[0, h] = (rope(q[:, sl]) * Q_PRESCALE).astype(BF16)
    k = jnp.dot(xn, w_ref[:, DIFF_QK:2 * DIFF_QK], preferred_element_type=F32)
    for h in range(DIFF_HEADS):
        sl = slice(h * LANES, (h + 1) * LANES)
        kr = rope(k[:, sl])
        k_ref[0, :, sl] = kr
        k16_ref[0, h] = kr.astype(BF16)
    v = jnp.dot(xn, w_ref[:, 2 * DIFF_QK:], preferred_element_type=F32)
    v_ref[0] = v
    for h in range(DIFF_HEADS):
        v16_ref[0, h] = v[:, h * LANES:(h + 1) * LANES].astype(BF16)


def _qkv_call(x, g, w, tabs, tm):
    B, T, _ = x.shape
    row = pl.BlockSpec((1, tm, D_MODEL), lambda b, i: (b, i, 0))
    heads = pl.BlockSpec((1, DIFF_HEADS, tm, LANES), lambda b, i: (b, 0, i, 0))
    tab = pl.BlockSpec((tm, LANES), lambda b, i: (i, 0))
    head_major = jax.ShapeDtypeStruct((B, DIFF_HEADS, T, LANES), BF16)
    return pl.pallas_call(
        _qkv_body,
        grid=(B, T // tm),
        in_specs=[row, _const_spec((1, D_MODEL)), _const_spec(w.shape), tab, tab, tab],
        out_specs=[row, row, heads, heads, heads],
        out_shape=[jax.ShapeDtypeStruct((B, T, DIFF_QK), F32),
                   jax.ShapeDtypeStruct((B, T, DIFF_VAL), F32),
                   head_major, head_major, head_major],
        compiler_params=_params(2),
        name="qkv_proj",
    )(x, g, w, *tabs)


def _rope_tables(pos):
    half = ROT_DIM // 2
    lane = jnp.arange(LANES, dtype=jnp.int32) % DIFF_HD
    inv = ROPE_THETA ** (-(2 * (lane % half)).astype(F32) / ROT_DIM)
    inv = jnp.where(lane < ROT_DIM, inv, 0.0)
    ang = pos.astype(F32)[:, None] * inv[None, :]
    cos, sin = jnp.cos(ang), jnp.sin(ang)
    s1 = jnp.where(lane[None, :] < half, -sin, 0.0)
    s2 = jnp.where((lane[None, :] >= half) & (lane[None, :] < ROT_DIM), sin, 0.0)
    return cos, s1, s2


def _gla_body(x_ref, g_ref, w_ref, wa_ref, ba_ref, s0_ref, o_ref, s_ref, st_ref, *, chunk):
    t = pl.program_id(1)
    C = chunk

    @pl.when(t == 0)
    def _():
        for h in range(GLA_HEADS):
            st_ref[h] = s0_ref[0, h].T

    ri = lax.broadcasted_iota(jnp.int32, (C, C), 0)
    ci = lax.broadcasted_iota(jnp.int32, (C, C), 1)
    tril = ri >= ci
    tril_b = jnp.where(tril, 1.0, 0.0).astype(BF16)

    heads = range(GLA_HEADS)
    ks = [slice(h * GLA_DK, (h + 1) * GLA_DK) for h in heads]
    vs = [slice(h * GLA_DV, (h + 1) * GLA_DV) for h in heads]

    n_chunks = x_ref.shape[1] // C
    xn = _rms_rows(x_ref[0], g_ref[...]).astype(BF16)
    u_all = jnp.dot(xn, w_ref[...], preferred_element_type=F32)
    z = jnp.dot(u_all[:, 2 * GLA_KEY + GLA_VAL:].astype(BF16), wa_ref[...],
                preferred_element_type=F32) + ba_ref[...]
    la_all = (jnp.minimum(z, 0.0) - jnp.log(1.0 + jnp.exp(-jnp.abs(z)))) * (1.0 / GLA_TAU)

    def operands(c):
        u, la = u_all[c * C:(c + 1) * C], la_all[c * C:(c + 1) * C]
        hi = la.astype(BF16)
        lo = (la - hi.astype(F32)).astype(BF16)
        b = (jnp.dot(tril_b, hi, preferred_element_type=F32)
             + jnp.dot(tril_b, lo, preferred_element_type=F32))
        r = b[C // 2 - 1:C // 2]
        bl = b[C - 1:C]
        q1 = u[:, 0:GLA_KEY] * GLA_SCALE * jnp.exp(b - r)
        qs = (q1 * jnp.exp(r)).astype(BF16)
        k1 = u[:, GLA_KEY:2 * GLA_KEY] * jnp.exp(r - b)
        kd = (k1 * jnp.exp(bl - r)).astype(BF16)
        v = u[:, 2 * GLA_KEY:2 * GLA_KEY + GLA_VAL].astype(BF16)
        return q1.astype(BF16), qs, k1.astype(BF16), kd, v, jnp.exp(bl)

    nxt = operands(0)
    for c in range(n_chunks):
        q1, qs, k1, kd, v, dec = nxt
        rows = slice(c * C, (c + 1) * C)
        a_raw = [lax.dot_general(q1[:, ks[h]], k1[:, ks[h]], _NT, preferred_element_type=F32) for h in heads]
        st = [st_ref[h] for h in heads]
        o_inter = [lax.dot_general(qs[:, ks[h]], st[h].astype(BF16), _NT, preferred_element_type=F32)
                   for h in heads]
        kv = [lax.dot_general(v[:, vs[h]], kd[:, ks[h]], _TN, preferred_element_type=F32) for h in heads]
        if c + 1 < n_chunks:
            nxt = operands(c + 1)
        for h in heads:
            a = jnp.where(tril, a_raw[h], 0.0).astype(BF16)
            o_ref[0, rows, vs[h]] = jnp.dot(a, v[:, vs[h]], preferred_element_type=F32) + o_inter[h]
            st_ref[h] = st[h] * dec[:, ks[h]] + kv[h]

    @pl.when(t == pl.num_programs(1) - 1)
    def _():
        for h in range(GLA_HEADS):
            s_ref[0, h] = st_ref[h].T


def _gla_call(x, g, w, wa, ba, s0, layer, tt, chunk):
    B, T, _ = x.shape
    row = pl.BlockSpec((1, tt, D_MODEL), lambda b, i: (b, i, 0))
    state_shape = (1, GLA_HEADS, GLA_DK, GLA_DV)
    return pl.pallas_call(
        functools.partial(_gla_body, chunk=chunk),
        grid=(B, T // tt),
        in_specs=[row, _const_spec((1, D_MODEL)), _const_spec(w.shape), _const_spec(wa.shape),
                  _const_spec((1, GLA_KEY)), pl.BlockSpec(state_shape, lambda b, i: (layer * B + b, 0, 0, 0))],
        out_specs=[row, pl.BlockSpec(state_shape, lambda b, i: (b, 0, 0, 0))],
        out_shape=[jax.ShapeDtypeStruct((B, T, GLA_VAL), F32),
                   jax.ShapeDtypeStruct((B, GLA_HEADS, GLA_DK, GLA_DV), F32)],
        scratch_shapes=[pltpu.VMEM((GLA_HEADS, GLA_DV, GLA_DK), F32)],
        compiler_params=_params(2),
        name="gla",
    )(x, g, w, wa, ba, s0)


def _lam(lq1, lk1, lq2, lk2, lambda_init):
    return (jnp.exp(jnp.sum(lq1[...] * lk1[...], axis=-1, keepdims=True))
            - jnp.exp(jnp.sum(lq2[...] * lk2[...], axis=-1, keepdims=True)) + lambda_init)


def _split_maps(q):
    lane = lax.broadcasted_iota(jnp.int32, q.shape, 1)
    zero = jnp.zeros_like(q)
    return jnp.where(lane < DIFF_HD, q, zero), jnp.where(lane >= DIFF_HD, q, zero)


def _ones_column(rows):
    lane = lax.broadcasted_iota(jnp.int32, (rows, LANES), 1)
    return jnp.where(lane == 0, 1.0, 0.0).astype(BF16)


def _attn_body(lq1, lk1, lq2, lk2, q_ref, k_ref, v_ref, bias_ref, *rest, n, tq, lambda_init):
    o_ref, s_ref = rest[-2:]
    nblk = tq // LANES
    hq = tq // 2
    qz = _split_maps(q_ref[0, 0])

    def lane_fold(s):
        return functools.reduce(jnp.maximum, [s[:, c * LANES:(c + 1) * LANES] for c in range(s.shape[1] // LANES)])

    mx = [None, None]
    for j in range(n):
        kt = k_ref[0, 0, j * tq:(j + 1) * tq, :]
        for m in range(2):
            if j < n - 1:
                s = lax.dot_general(qz[m], kt, _NT, preferred_element_type=F32)
                s_ref[m, j] = s
                fold = lane_fold(s)
            else:
                top = lax.dot_general(qz[m][:hq], kt[:hq], _NT, preferred_element_type=F32) + bias_ref[:hq, :hq]
                bot = lax.dot_general(qz[m][hq:], kt, _NT, preferred_element_type=F32) + bias_ref[hq:, :]
                s_ref[m, j, :hq, :hq] = top
                s_ref[m, j, hq:, :] = bot
                fold = jnp.concatenate([lane_fold(top), lane_fold(bot)], axis=0)
            mx[m] = fold if mx[m] is None else jnp.maximum(mx[m], fold)
    ones = _ones_column(tq)
    acc = []
    for m in range(2):
        mrow = jnp.broadcast_to(jnp.max(mx[m], axis=-1, keepdims=True), (tq, LANES))
        mrow = jnp.concatenate([mrow] * nblk, axis=1)
        a = None
        for j in range(n):
            vt = jnp.concatenate([v_ref[0, 0, j * tq:(j + 1) * tq, :], ones], axis=1)
            if j < n - 1:
                p = jnp.exp2(s_ref[m, j] - mrow).astype(BF16)
                d = jnp.dot(p, vt, preferred_element_type=F32)
            else:
                p_top = jnp.exp2(s_ref[m, j, :hq, :hq] - mrow[:hq, :hq]).astype(BF16)
                p_bot = jnp.exp2(s_ref[m, j, hq:, :] - mrow[hq:]).astype(BF16)
                d = jnp.concatenate([jnp.dot(p_top, vt[:hq], preferred_element_type=F32),
                                     jnp.dot(p_bot, vt, preferred_element_type=F32)], axis=0)
            a = d if a is None else a + d
        acc.append(a)
    lam = _lam(lq1, lk1, lq2, lk2, lambda_init)
    o = (acc[0][:, 0:DIFF_VD] / acc[0][:, DIFF_VD:DIFF_VD + 1]
         - lam * (acc[1][:, 0:DIFF_VD] / acc[1][:, DIFF_VD:DIFF_VD + 1])).astype(o_ref.dtype)
    if o_ref.shape[1] == tq:
        o_ref[0] = o
    else:
        o_ref[0, (n - 1) * tq:n * tq] = o
        o_ref[0, 0:(n - 1) * tq] = jnp.zeros(((n - 1) * tq, LANES), o_ref.dtype)


def _attn_call(lams, q, k, v, bias, o_prev, n, tq, lambda_init):
    B, _, T, _ = q.shape
    assert tq % (2 * LANES) == 0 and (tq // 2) % CHUNK == 0, tq
    lam_spec = _const_spec((1, DIFF_HD))
    qs = pl.BlockSpec((1, 1, tq, LANES), lambda b, h: (b, h, n - 1, 0))
    kv = pl.BlockSpec((1, 1, n * tq, LANES), lambda b, h: (b, h, 0, 0))
    in_specs = [lam_spec] * 4 + [qs, kv, kv, _const_spec((tq, tq))]
    args = (*lams, q, k, v, bias)
    if o_prev is None:
        assert n * tq == T, (n, tq, T)
        aliases = {}
        out_spec = pl.BlockSpec((1, T, LANES), lambda b, h: (b, 0, h))
    else:
        in_specs.append(pl.BlockSpec(memory_space=pl.ANY))
        aliases = {len(args): 0}
        args = args + (o_prev,)
        out_spec = pl.BlockSpec((1, tq, LANES), lambda b, h: (b, n - 1, h))
    return pl.pallas_call(
        functools.partial(_attn_body, n=n, tq=tq, lambda_init=lambda_init),
        grid=(B, DIFF_HEADS),
        in_specs=in_specs,
        out_specs=out_spec,
        out_shape=jax.ShapeDtypeStruct((B, T, DIFF_VAL), BF16),
        scratch_shapes=[pltpu.VMEM((2, n, tq, tq), F32)],
        input_output_aliases=aliases,
        compiler_params=_params(2),
        name=f"diff_attn_q{n - 1}",
    )(*args)


def _diag_bias(tq):
    r = jnp.arange(tq, dtype=jnp.int32) // CHUNK
    return jnp.where(r[:, None] >= r[None, :], 0.0, -jnp.inf).astype(F32)


def _attn_cache_body(lq1, lk1, lq2, lk2, q_ref, kc_ref, vc_ref, kn_ref, vn_ref, o_ref, *, lambda_init):
    T = q_ref.shape[2]
    P = kc_ref.shape[1] // DIFF_HEADS
    lam = _lam(lq1, lk1, lq2, lk2, lambda_init)
    for h in range(DIFF_HEADS):
        qz = jnp.concatenate(_split_maps(q_ref[0, h]), axis=0)
        kc = kc_ref[0, pl.ds(h, P, stride=DIFF_HEADS), :].astype(BF16)
        vc = vc_ref[0, pl.ds(h, P, stride=DIFF_HEADS), :].astype(BF16)
        sc = lax.dot_general(qz, kc, _NT, preferred_element_type=F32)
        sn = lax.dot_general(qz, kn_ref[0, h], _NT, preferred_element_type=F32)
        mx = jnp.maximum(jnp.max(sc, axis=-1, keepdims=True), jnp.max(sn, axis=-1, keepdims=True))
        pc = jnp.exp2(sc - mx)
        pn = jnp.exp2(sn - mx)
        l = jnp.sum(pc, axis=-1, keepdims=True) + jnp.sum(pn, axis=-1, keepdims=True)
        o = (jnp.dot(pc.astype(BF16), vc, preferred_element_type=F32)
             + jnp.dot(pn.astype(BF16), vn_ref[0, h], preferred_element_type=F32)) / l
        o_ref[0, :, h * DIFF_VD:(h + 1) * DIFF_VD] = o[0:T] - lam * o[T:2 * T]


def _attn_cache_call(lams, q, kc, vc, kn, vn, layer, lambda_init):
    B, _, T, _ = q.shape
    lam_spec = _const_spec((1, DIFF_HD))
    new = pl.BlockSpec((1, DIFF_HEADS, T, LANES), lambda b: (b, 0, 0, 0))
    past = pl.BlockSpec((1, kc.shape[1], LANES), lambda b: (layer * B + b, 0, 0))
    return pl.pallas_call(
        functools.partial(_attn_cache_body, lambda_init=lambda_init),
        grid=(B,),
        in_specs=[lam_spec] * 4 + [new, past, past, new, new],
        out_specs=pl.BlockSpec((1, T, DIFF_VAL), lambda b: (b, 0, 0)),
        out_shape=jax.ShapeDtypeStruct((B, T, DIFF_VAL), F32),
        compiler_params=_params(1),
        name="diff_attn_cache",
    )(*lams, q, kc, vc, kn, vn)


def _merge_body(x_ref, og_ref, od_ref, gpre_ref, wg_ref, ggla_ref, gsub_ref, wpg_ref, wpd_ref, wo_ref,
                gpost_ref, y_ref, yg_s, yd_s, *, lambda_init):
    x = x_ref[...]
    xn = _rms_rows(x, gpre_ref[...]).astype(BF16)
    gates = jnp.dot(xn, wg_ref[...], preferred_element_type=F32)
    og, od = og_ref[...], od_ref[...].astype(F32)
    for h in range(GLA_HEADS):
        sl = slice(h * GLA_DV, (h + 1) * GLA_DV)
        yg_s[:, sl] = (_rms_rows(og[:, sl], ggla_ref[...]) * jax.nn.silu(gates[:, sl])).astype(BF16)
    for h in range(DIFF_HEADS):
        sl = slice(h * DIFF_VD, (h + 1) * DIFF_VD)
        dg = gates[:, GLA_VAL + h * DIFF_VD:GLA_VAL + (h + 1) * DIFF_VD]
        yd_s[:, sl] = (_rms_rows(od[:, sl], gsub_ref[...]) * (1.0 - lambda_init) * jax.nn.silu(dg)).astype(BF16)
    ma = gates[:, GLA_VAL + DIFF_VAL:GLA_VAL + DIFF_VAL + D_MODEL]
    mb = gates[:, GLA_VAL + DIFF_VAL + D_MODEL:]
    merged = (jax.nn.sigmoid(ma) * jnp.dot(yg_s[...], wpg_ref[...], preferred_element_type=F32)
              + jax.nn.sigmoid(mb) * jnp.dot(yd_s[...], wpd_ref[...], preferred_element_type=F32))
    z = jnp.dot(merged.astype(BF16), wo_ref[...], preferred_element_type=F32)
    y_ref[...] = x + _rms_rows(z, gpost_ref[...])


def _merge_call(x, og, od, gpre, wg, ggla, gsub, wpg, wpd, wo, gpost, tm, lambda_init):
    N = x.shape[0]
    row = pl.BlockSpec((tm, D_MODEL), lambda i: (i, 0))
    return pl.pallas_call(
        functools.partial(_merge_body, lambda_init=lambda_init),
        grid=(N // tm,),
        in_specs=[row, row, row, _const_spec((1, D_MODEL)), _const_spec(wg.shape), _const_spec((1, GLA_DV)),
                  _const_spec((1, DIFF_VD)), _const_spec(wpg.shape), _const_spec(wpd.shape),
                  _const_spec(wo.shape), _const_spec((1, D_MODEL))],
        out_specs=row,
        out_shape=jax.ShapeDtypeStruct((N, D_MODEL), F32),
        scratch_shapes=[pltpu.VMEM((tm, GLA_VAL), BF16), pltpu.VMEM((tm, DIFF_VAL), BF16)],
        compiler_params=_params(1),
        name="merge",
    )(x, og, od, gpre, wg, ggla, gsub, wpg, wpd, wo, gpost)


def _tile(n, target):
    t = min(n, target)
    while n % t:
        t -= 1
    return t


def kernel(x_prompt, x_sample, cache_diff_k, cache_diff_v, state_gla, pre_norm_g, w_in, gla_w_a2, gla_b_a,
           gla_norm_g, diff_lambda_q1, diff_lambda_k1, diff_lambda_q2, diff_lambda_k2, diff_subln_g,
           w_proj_gla, w_proj_diff, w_out, post_norm_g):
    depth = w_in.shape[0]
    past_len = cache_diff_k.shape[2]
    hp, hs = x_prompt, x_sample
    B, T, _ = hp.shape
    Bs, Ts, _ = hs.shape
    cache_k = cache_diff_k.reshape(depth * Bs, past_len * DIFF_HEADS, 2 * DIFF_HD)
    cache_v = cache_diff_v.reshape(depth * Bs, past_len * DIFF_HEADS, DIFF_VD)
    state_in = state_gla.astype(F32).reshape(depth * Bs, GLA_HEADS, GLA_DK, GLA_DV)
    zero_state = jnp.zeros((B, GLA_HEADS, GLA_DK, GLA_DV), F32)
    outs = [[] for _ in range(6)]
    offs = [0]
    for s in IN_SIZES:
        offs.append(offs[-1] + s)
    for l in range(depth):
        lambda_init = 0.8 - 0.6 * math.exp(-0.3 * l)
        def cols(first, last):
            return w_in[l, :, offs[first]:offs[last + 1]].astype(BF16)

        w_gla = jnp.concatenate([cols(0, 2), jnp.pad(cols(3, 3), ((0, 0), (0, LANES - GLA_RANK)))], axis=1)
        w_qkv = cols(5, 7)
        w_gate = jnp.concatenate([cols(4, 4), cols(8, 10)], axis=1)
        wa = jnp.pad(gla_w_a2[l], ((0, LANES - GLA_RANK), (0, 0))).astype(BF16)
        ba = gla_b_a[l][None]
        gpre = pre_norm_g[l][None]
        lams = (diff_lambda_q1[l][None], diff_lambda_k1[l][None], diff_lambda_q2[l][None], diff_lambda_k2[l][None])
        merge_w = (gpre, w_gate, gla_norm_g[l][None], diff_subln_g[l][None], w_proj_gla[l].astype(BF16),
                   w_proj_diff[l].astype(BF16), w_out[l].astype(BF16), post_norm_g[l][None])

        k, v, q16, k16, v16 = _qkv_call(hp, gpre, w_qkv, _rope_tables(jnp.arange(T, dtype=jnp.int32)),
                                        _tile(T, 512))
        o_gla, s_p = _gla_call(hp, gpre, w_gla, wa, ba, zero_state, 0, _tile(T, 512), _tile(T, 128))
        tq = _tile(T, 512)
        bias = _diag_bias(tq)
        o_diff = None
        for n in range(T // tq, 0, -1):
            o_diff = _attn_call(lams, q16, k16, v16, bias, o_diff, n, tq, lambda_init)
        hp = _merge_call(hp.reshape(B * T, D_MODEL), o_gla.reshape(B * T, GLA_VAL),
                         o_diff.reshape(B * T, DIFF_VAL), *merge_w, _tile(B * T, 512),
                         lambda_init).reshape(B, T, D_MODEL)
        outs[0].append(k.reshape(B, T, DIFF_HEADS, DIFF_VD))
        outs[1].append(v.reshape(B, T, DIFF_HEADS, DIFF_VD))
        outs[2].append(s_p)

        pos_s = past_len + jnp.arange(Ts, dtype=jnp.int32)
        k, v, q16, k16, v16 = _qkv_call(hs, gpre, w_qkv, _rope_tables(pos_s), Ts)
        o_gla, s_s = _gla_call(hs, gpre, w_gla, wa, ba, state_in, l, Ts, Ts)
        o_diff = _attn_cache_call(lams, q16, cache_k, cache_v, k16, v16, l, lambda_init)
        hs = _merge_call(hs.reshape(Bs * Ts, D_MODEL), o_gla.reshape(Bs * Ts, GLA_VAL),
                         o_diff.reshape(Bs * Ts, DIFF_VAL), *merge_w, _tile(Bs * Ts, 256),
                         lambda_init).reshape(Bs, Ts, D_MODEL)
        outs[3].append(k.reshape(Bs, Ts, DIFF_HEADS, DIFF_VD))
        outs[4].append(v.reshape(Bs, Ts, DIFF_HEADS, DIFF_VD))
        outs[5].append(s_s)

    return (hp, hs, jnp.stack(outs[0]), jnp.stack(outs[1]), jnp.stack(outs[2]),
            jnp.stack(outs[3]), jnp.stack(outs[4]), jnp.stack(outs[5]))
```

```python
import functools
import math

import jax
import jax.numpy as jnp
from jax import lax
from jax.experimental import pallas as pl
from jax.experimental.pallas import tpu as pltpu

F32 = jnp.float32
BF16 = jnp.bfloat16

D_MODEL = 1024
EPS = 1e-6
CHUNK = 64

GLA_HEADS = 4
GLA_DK = 128
GLA_DV = 256
GLA_KEY = GLA_HEADS * GLA_DK
GLA_VAL = GLA_HEADS * GLA_DV
GLA_RANK = 16
GLA_TAU = 16.0
GLA_SCALE = GLA_DK ** -0.5

DIFF_HEADS = 8
DIFF_HD = 64
DIFF_VD = 128
DIFF_QK = DIFF_HEADS * 2 * DIFF_HD
DIFF_VAL = DIFF_HEADS * DIFF_VD
ROT_DIM = DIFF_HD // 4
ROPE_THETA = 500000.0
QK_SCALE = DIFF_HD ** -0.5
Q_PRESCALE = QK_SCALE * math.log2(math.e)

IN_SIZES = (GLA_KEY, GLA_KEY, GLA_VAL, GLA_RANK, GLA_VAL,
            DIFF_QK, DIFF_QK, DIFF_VAL, DIFF_VAL, D_MODEL, D_MODEL)

LANES = 128
VMEM_LIMIT = 56 * 1024 * 1024

_NT = (((1,), (1,)), ((), ()))
_TN = (((0,), (0,)), ((), ()))


def _rms_rows(x, g):
    return x * lax.rsqrt(jnp.mean(x * x, axis=-1, keepdims=True) + EPS) * g


def _params(n_axes):
    return pltpu.CompilerParams(dimension_semantics=("arbitrary",) * n_axes,
                                vmem_limit_bytes=VMEM_LIMIT)


def _const_spec(shape):
    zeros = (0,) * len(shape)
    return pl.BlockSpec(shape, lambda *_: zeros, pipeline_mode=pl.Buffered(1))


def _qkv_body(x_ref, g_ref, w_ref, c_ref, s1_ref, s2_ref, k_ref, v_ref, q16_ref, k16_ref, v16_ref):
    xn = _rms_rows(x_ref[0], g_ref[...]).astype(BF16)
    c, s1, s2 = c_ref[...], s1_ref[...], s2_ref[...]

    def rope(t):
        return t * c + pltpu.roll(t, LANES - ROT_DIM // 2, 1) * s1 + pltpu.roll(t, ROT_DIM // 2, 1) * s2

    q = jnp.dot(xn, w_ref[:, 0:DIFF_QK], preferred_element_type=F32)
    for h in range(DIFF_HEADS):
        sl = slice(h * LANES, (h + 1) * LANES)
        q16_ref[0, h] = (rope(q[:, sl]) * Q_PRESCALE).astype(BF16)
    k = jnp.dot(xn, w_ref[:, DIFF_QK:2 * DIFF_QK], preferred_element_type=F32)
    for h in range(DIFF_HEADS):
        sl = slice(h * LANES, (h + 1) * LANES)
        kr = rope(k[:, sl])
        k_ref[0, :, sl] = kr
        k16_ref[0, h] = kr.astype(BF16)
    v = jnp.dot(xn, w_ref[:, 2 * DIFF_QK:], preferred_element_type=F32)
    v_ref[0] = v
    for h in range(DIFF_HEADS):
        v16_ref[0, h] = v[:, h * LANES:(h + 1) * LANES].astype(BF16)


def _qkv_call(x, g, w, tabs, tm):
    B, T, _ = x.shape
    row = pl.BlockSpec((1, tm, D_MODEL), lambda b, i: (b, i, 0))
    heads = pl.BlockSpec((1, DIFF_HEADS, tm, LANES), lambda b, i: (b, 0, i, 0))
    tab = pl.BlockSpec((tm, LANES), lambda b, i: (i, 0))
    head_major = jax.ShapeDtypeStruct((B, DIFF_HEADS, T, LANES), BF16)
    return pl.pallas_call(
        _qkv_body,
        grid=(B, T // tm),
        in_specs=[row, _const_spec((1, D_MODEL)), _const_spec(w.shape), tab, tab, tab],
        out_specs=[row, row, heads, heads, heads],
        out_shape=[jax.ShapeDtypeStruct((B, T, DIFF_QK), F32),
                   jax.ShapeDtypeStruct((B, T, DIFF_VAL), F32),
                   head_major, head_major, head_major],
        compiler_params=_params(2),
        name="qkv_proj",
    )(x, g, w, *tabs)


def _rope_tables(pos):
    half = ROT_DIM // 2
    lane = jnp.arange(LANES, dtype=jnp.int32) % DIFF_HD
    inv = ROPE_THETA ** (-(2 * (lane % half)).astype(F32) / ROT_DIM)
    inv = jnp.where(lane < ROT_DIM, inv, 0.0)
    ang = pos.astype(F32)[:, None] * inv[None, :]
    cos, sin = jnp.cos(ang), jnp.sin(ang)
    s1 = jnp.where(lane[None, :] < half, -sin, 0.0)
    s2 = jnp.where((lane[None, :] >= half) & (lane[None, :] < ROT_DIM), sin, 0.0)
    return cos, s1, s2


def _gla_body(x_ref, g_ref, w_ref, wa_ref, ba_ref, s0_ref, o_ref, s_ref, st_ref, *, chunk):
    t = pl.program_id(1)
    C = chunk

    @pl.when(t == 0)
    def _():
        for h in range(GLA_HEADS):
            st_ref[h] = s0_ref[0, h].T

    ri = lax.broadcasted_iota(jnp.int32, (C, C), 0)
    ci = lax.broadcasted_iota(jnp.int32, (C, C), 1)
    tril = ri >= ci
    tril_b = jnp.where(tril, 1.0, 0.0).astype(BF16)

    heads = range(GLA_HEADS)
    ks = [slice(h * GLA_DK, (h + 1) * GLA_DK) for h in heads]
    vs = [slice(h * GLA_DV, (h + 1) * GLA_DV) for h in heads]

    n_chunks = x_ref.shape[1] // C
    xn = _rms_rows(x_ref[0], g_ref[...]).astype(BF16)
    u_all = jnp.dot(xn, w_ref[...], preferred_element_type=F32)
    z = jnp.dot(u_all[:, 2 * GLA_KEY + GLA_VAL:].astype(BF16), wa_ref[...],
                preferred_element_type=F32) + ba_ref[...]
    la_all = (jnp.minimum(z, 0.0) - jnp.log(1.0 + jnp.exp(-jnp.abs(z)))) * (1.0 / GLA_TAU)

    def operands(c):
        u, la = u_all[c * C:(c + 1) * C], la_all[c * C:(c + 1) * C]
        hi = la.astype(BF16)
        lo = (la - hi.astype(F32)).astype(BF16)
        b = (jnp.dot(tril_b, hi, preferred_element_type=F32)
             + jnp.dot(tril_b, lo, preferred_element_type=F32))
        r = b[C // 2 - 1:C // 2]
        bl = b[C - 1:C]
        q1 = u[:, 0:GLA_KEY] * GLA_SCALE * jnp.exp(b - r)
        qs = (q1 * jnp.exp(r)).astype(BF16)
        k1 = u[:, GLA_KEY:2 * GLA_KEY] * jnp.exp(r - b)
        kd = (k1 * jnp.exp(bl - r)).astype(BF16)
        v = u[:, 2 * GLA_KEY:2 * GLA_KEY + GLA_VAL].astype(BF16)
        return q1.astype(BF16), qs, k1.astype(BF16), kd, v, jnp.exp(bl)

    nxt = operands(0)
    for c in range(n_chunks):
        q1, qs, k1, kd, v, dec = nxt
        rows = slice(c * C, (c + 1) * C)
        a_raw = [lax.dot_general(q1[:, ks[h]], k1[:, ks[h]], _NT, preferred_element_type=F32) for h in heads]
        st = [st_ref[h] for h in heads]
        o_inter = [lax.dot_general(qs[:, ks[h]], st[h].astype(BF16), _NT, preferred_element_type=F32)
                   for h in heads]
        kv = [lax.dot_general(v[:, vs[h]], kd[:, ks[h]], _TN, preferred_element_type=F32) for h in heads]
        if c + 1 < n_chunks:
            nxt = operands(c + 1)
        for h in heads:
            a = jnp.where(tril, a_raw[h], 0.0).astype(BF16)
            o_ref[0, rows, vs[h]] = jnp.dot(a, v[:, vs[h]], preferred_element_type=F32) + o_inter[h]
            st_ref[h] = st[h] * dec[:, ks[h]] + kv[h]

    @pl.when(t == pl.num_programs(1) - 1)
    def _():
        for h in range(GLA_HEADS):
            s_ref[0, h] = st_ref[h].T


def _gla_call(x, g, w, wa, ba, s0, layer, tt, chunk):
    B, T, _ = x.shape
    row = pl.BlockSpec((1, tt, D_MODEL), lambda b, i: (b, i, 0))
    state_shape = (1, GLA_HEADS, GLA_DK, GLA_DV)
    return pl.pallas_call(
        functools.partial(_gla_body, chunk=chunk),
        grid=(B, T // tt),
        in_specs=[row, _const_spec((1, D_MODEL)), _const_spec(w.shape), _const_spec(wa.shape),
                  _const_spec((1, GLA_KEY)), pl.BlockSpec(state_shape, lambda b, i: (layer * B + b, 0, 0, 0))],
        out_specs=[row, pl.BlockSpec(state_shape, lambda b, i: (b, 0, 0, 0))],
        out_shape=[jax.ShapeDtypeStruct((B, T, GLA_VAL), F32),
                   jax.ShapeDtypeStruct((B, GLA_HEADS, GLA_DK, GLA_DV), F32)],
        scratch_shapes=[pltpu.VMEM((GLA_HEADS, GLA_DV, GLA_DK), F32)],
        compiler_params=_params(2),
        name="gla",
    )(x, g, w, wa, ba, s0)


def _lam(lq1, lk1, lq2, lk2, lambda_init):
    return (jnp.exp(jnp.sum(lq1[...] * lk1[...], axis=-1, keepdims=True))
            - jnp.exp(jnp.sum(lq2[...] * lk2[...], axis=-1, keepdims=True)) + lambda_init)


def _split_maps(q):
    lane = lax.broadcasted_iota(jnp.int32, q.shape, 1)
    zero = jnp.zeros_like(q)
    return jnp.where(lane < DIFF_HD, q, zero), jnp.where(lane >= DIFF_HD, q, zero)


def _ones_column(rows):
    lane = lax.broadcasted_iota(jnp.int32, (rows, LANES), 1)
    return jnp.where(lane == 0, 1.0, 0.0).astype(BF16)


def _attn_body(lq1, lk1, lq2, lk2, q_ref, k_ref, v_ref, bias_ref, *rest, n, tq, lambda_init):
    o_ref, s_ref = rest[-2:]
    heads = q_ref.shape[1]
    nblk = tq // LANES
    hq = tq // 2

    def lane_fold(s):
        return functools.reduce(jnp.maximum, [s[:, c * LANES:(c + 1) * LANES] for c in range(s.shape[1] // LANES)])

    mx = {}
    for g in range(heads):
        qz = _split_maps(q_ref[0, g])
        for j in range(n):
            kt = k_ref[0, g, j * tq:(j + 1) * tq, :]
            for m in range(2):
                if j < n - 1:
                    s = lax.dot_general(qz[m], kt, _NT, preferred_element_type=F32)
                    s_ref[g, m, j] = s
                    fold = lane_fold(s)
                else:
                    top = (lax.dot_general(qz[m][:hq], kt[:hq], _NT, preferred_element_type=F32)
                           + bias_ref[:hq, :hq])
                    bot = lax.dot_general(qz[m][hq:], kt, _NT, preferred_element_type=F32) + bias_ref[hq:, :]
                    s_ref[g, m, j, :hq, :hq] = top
                    s_ref[g, m, j, hq:, :] = bot
                    fold = jnp.concatenate([lane_fold(top), lane_fold(bot)], axis=0)
                mx[g, m] = fold if j == 0 else jnp.maximum(mx[g, m], fold)
    ones = _ones_column(tq)
    lam = _lam(lq1, lk1, lq2, lk2, lambda_init)
    for g in range(heads):
        acc = []
        for m in range(2):
            mrow = jnp.broadcast_to(jnp.max(mx[g, m], axis=-1, keepdims=True), (tq, LANES))
            mrow = jnp.concatenate([mrow] * nblk, axis=1)
            a = None
            for j in range(n):
                vt = jnp.concatenate([v_ref[0, g, j * tq:(j + 1) * tq, :], ones], axis=1)
                if j < n - 1:
                    p = jnp.exp2(s_ref[g, m, j] - mrow).astype(BF16)
                    d = jnp.dot(p, vt, preferred_element_type=F32)
                else:
                    p_top = jnp.exp2(s_ref[g, m, j, :hq, :hq] - mrow[:hq, :hq]).astype(BF16)
                    p_bot = jnp.exp2(s_ref[g, m, j, hq:, :] - mrow[hq:]).astype(BF16)
                    d = jnp.concatenate([jnp.dot(p_top, vt[:hq], preferred_element_type=F32),
                                         jnp.dot(p_bot, vt, preferred_element_type=F32)], axis=0)
                a = d if a is None else a + d
            acc.append(a)
        o = (acc[0][:, 0:DIFF_VD] / acc[0][:, DIFF_VD:DIFF_VD + 1]
             - lam * (acc[1][:, 0:DIFF_VD] / acc[1][:, DIFF_VD:DIFF_VD + 1])).astype(o_ref.dtype)
        cols = slice(g * DIFF_VD, (g + 1) * DIFF_VD)
        if o_ref.shape[1] == tq:
            o_ref[0, :, cols] = o
        else:
            o_ref[0, (n - 1) * tq:n * tq, cols] = o
            o_ref[0, 0:(n - 1) * tq, cols] = jnp.zeros(((n - 1) * tq, DIFF_VD), o_ref.dtype)


ATTN_SCORE_BYTES = 32 * 1024 * 1024


def _attn_call(lams, q, k, v, bias, o_prev, n, tq, lambda_init):
    B, _, T, _ = q.shape
    assert tq % (2 * LANES) == 0 and (tq // 2) % CHUNK == 0, tq
    heads = DIFF_HEADS
    while heads > 1 and heads * 2 * n * tq * tq * 4 > ATTN_SCORE_BYTES:
        heads //= 2
    lam_spec = _const_spec((1, DIFF_HD))
    qs = pl.BlockSpec((1, heads, tq, LANES), lambda b, h: (b, h, n - 1, 0))
    kv = pl.BlockSpec((1, heads, n * tq, LANES), lambda b, h: (b, h, 0, 0))
    in_specs = [lam_spec] * 4 + [qs, kv, kv, _const_spec((tq, tq))]
    args = (*lams, q, k, v, bias)
    if o_prev is None:
        assert n * tq == T, (n, tq, T)
        aliases = {}
        out_spec = pl.BlockSpec((1, T, heads * DIFF_VD), lambda b, h: (b, 0, h))
    else:
        in_specs.append(pl.BlockSpec(memory_space=pl.ANY))
        aliases = {len(args): 0}
        args = args + (o_prev,)
        out_spec = pl.BlockSpec((1, tq, heads * DIFF_VD), lambda b, h: (b, n - 1, h))
    return pl.pallas_call(
        functools.partial(_attn_body, n=n, tq=tq, lambda_init=lambda_init),
        grid=(B, DIFF_HEADS // heads),
        in_specs=in_specs,
        out_specs=out_spec,
        out_shape=jax.ShapeDtypeStruct((B, T, DIFF_VAL), BF16),
        scratch_shapes=[pltpu.VMEM((heads, 2, n, tq, tq), F32)],
        input_output_aliases=aliases,
        compiler_params=_params(2),
        name=f"diff_attn_q{n - 1}",
    )(*args)


def _diag_bias(tq):
    r = jnp.arange(tq, dtype=jnp.int32) // CHUNK
    return jnp.where(r[:, None] >= r[None, :], 0.0, -jnp.inf).astype(F32)


def _attn_cache_body(lq1, lk1, lq2, lk2, q_ref, kc_ref, vc_ref, kn_ref, vn_ref, o_ref, *, lambda_init):
    T = q_ref.shape[2]
    P = kc_ref.shape[1] // DIFF_HEADS
    lam = _lam(lq1, lk1, lq2, lk2, lambda_init)
    for h in range(DIFF_HEADS):
        qz = jnp.concatenate(_split_maps(q_ref[0, h]), axis=0)
        kc = kc_ref[0, pl.ds(h, P, stride=DIFF_HEADS), :].astype(BF16)
        vc = vc_ref[0, pl.ds(h, P, stride=DIFF_HEADS), :].astype(BF16)
        sc = lax.dot_general(qz, kc, _NT, preferred_element_type=F32)
        sn = lax.dot_general(qz, kn_ref[0, h], _NT, preferred_element_type=F32)
        mx = jnp.maximum(jnp.max(sc, axis=-1, keepdims=True), jnp.max(sn, axis=-1, keepdims=True))
        pc = jnp.exp2(sc - mx)
        pn = jnp.exp2(sn - mx)
        l = jnp.sum(pc, axis=-1, keepdims=True) + jnp.sum(pn, axis=-1, keepdims=True)
        o = (jnp.dot(pc.astype(BF16), vc, preferred_element_type=F32)
             + jnp.dot(pn.astype(BF16), vn_ref[0, h], preferred_element_type=F32)) / l
        o_ref[0, :, h * DIFF_VD:(h + 1) * DIFF_VD] = o[0:T] - lam * o[T:2 * T]


def _attn_cache_call(lams, q, kc, vc, kn, vn, layer, lambda_init):
    B, _, T, _ = q.shape
    lam_spec = _const_spec((1, DIFF_HD))
    new = pl.BlockSpec((1, DIFF_HEADS, T, LANES), lambda b: (b, 0, 0, 0))
    past = pl.BlockSpec((1, kc.shape[1], LANES), lambda b: (layer * B + b, 0, 0))
    return pl.pallas_call(
        functools.partial(_attn_cache_body, lambda_init=lambda_init),
        grid=(B,),
        in_specs=[lam_spec] * 4 + [new, past, past, new, new],
        out_specs=pl.BlockSpec((1, T, DIFF_VAL), lambda b: (b, 0, 0)),
        out_shape=jax.ShapeDtypeStruct((B, T, DIFF_VAL), F32),
        compiler_params=_params(1),
        name="diff_attn_cache",
    )(*lams, q, kc, vc, kn, vn)


def _merge_body(x_ref, og_ref, od_ref, gpre_ref, wg_ref, ggla_ref, gsub_ref, wpg_ref, wpd_ref, wo_ref,
                gpost_ref, y_ref, yg_s, yd_s, *, lambda_init):
    x = x_ref[...]
    xn = _rms_rows(x, gpre_ref[...]).astype(BF16)
    gates = jnp.dot(xn, wg_ref[...], preferred_element_type=F32)
    og, od = og_ref[...], od_ref[...].astype(F32)
    for h in range(GLA_HEADS):
        sl = slice(h * GLA_DV, (h + 1) * GLA_DV)
        yg_s[:, sl] = (_rms_rows(og[:, sl], ggla_ref[...]) * jax.nn.silu(gates[:, sl])).astype(BF16)
    for h in range(DIFF_HEADS):
        sl = slice(h * DIFF_VD, (h + 1) * DIFF_VD)
        dg = gates[:, GLA_VAL + h * DIFF_VD:GLA_VAL + (h + 1) * DIFF_VD]
        yd_s[:, sl] = (_rms_rows(od[:, sl], gsub_ref[...]) * (1.0 - lambda_init) * jax.nn.silu(dg)).astype(BF16)
    ma = gates[:, GLA_VAL + DIFF_VAL:GLA_VAL + DIFF_VAL + D_MODEL]
    mb = gates[:, GLA_VAL + DIFF_VAL + D_MODEL:]
    merged = (jax.nn.sigmoid(ma) * jnp.dot(yg_s[...], wpg_ref[...], preferred_element_type=F32)
              + jax.nn.sigmoid(mb) * jnp.dot(yd_s[...], wpd_ref[...], preferred_element_type=F32))
    z = jnp.dot(merged.astype(BF16), wo_ref[...], preferred_element_type=F32)
    y_ref[...] = x + _rms_rows(z, gpost_ref[...])


def _merge_call(x, og, od, gpre, wg, ggla, gsub, wpg, wpd, wo, gpost, tm, lambda_init):
    N = x.shape[0]
    row = pl.BlockSpec((tm, D_MODEL), lambda i: (i, 0))
    return pl.pallas_call(
        functools.partial(_merge_body, lambda_init=lambda_init),
        grid=(N // tm,),
        in_specs=[row, row, row, _const_spec((1, D_MODEL)), _const_spec(wg.shape), _const_spec((1, GLA_DV)),
                  _const_spec((1, DIFF_VD)), _const_spec(wpg.shape), _const_spec(wpd.shape),
                  _const_spec(wo.shape), _const_spec((1, D_MODEL))],
        out_specs=row,
        out_shape=jax.ShapeDtypeStruct((N, D_MODEL), F32),
        scratch_shapes=[pltpu.VMEM((tm, GLA_VAL), BF16), pltpu.VMEM((tm, DIFF_VAL), BF16)],
        compiler_params=_params(1),
        name="merge",
    )(x, og, od, gpre, wg, ggla, gsub, wpg, wpd, wo, gpost)


def _tile(n, target):
    t = min(n, target)
    while n % t:
        t -= 1
    return t


def kernel(x_prompt, x_sample, cache_diff_k, cache_diff_v, state_gla, pre_norm_g, w_in, gla_w_a2, gla_b_a,
           gla_norm_g, diff_lambda_q1, diff_lambda_k1, diff_lambda_q2, diff_lambda_k2, diff_subln_g,
           w_proj_gla, w_proj_diff, w_out, post_norm_g):
    depth = w_in.shape[0]
    past_len = cache_diff_k.shape[2]
    hp, hs = x_prompt, x_sample
    B, T, _ = hp.shape
    Bs, Ts, _ = hs.shape
    cache_k = cache_diff_k.reshape(depth * Bs, past_len * DIFF_HEADS, 2 * DIFF_HD)
    cache_v = cache_diff_v.reshape(depth * Bs, past_len * DIFF_HEADS, DIFF_VD)
    state_in = state_gla.astype(F32).reshape(depth * Bs, GLA_HEADS, GLA_DK, GLA_DV)
    zero_state = jnp.zeros((B, GLA_HEADS, GLA_DK, GLA_DV), F32)
    outs = [[] for _ in range(6)]
    offs = [0]
    for s in IN_SIZES:
        offs.append(offs[-1] + s)
    for l in range(depth):
        lambda_init = 0.8 - 0.6 * math.exp(-0.3 * l)
        def cols(first, last):
            return w_in[l, :, offs[first]:offs[last + 1]].astype(BF16)

        w_gla = jnp.concatenate([cols(0, 2), jnp.pad(cols(3, 3), ((0, 0), (0, LANES - GLA_RANK)))], axis=1)
        w_qkv = cols(5, 7)
        w_gate = jnp.concatenate([cols(4, 4), cols(8, 10)], axis=1)
        wa = jnp.pad(gla_w_a2[l], ((0, LANES - GLA_RANK), (0, 0))).astype(BF16)
        ba = gla_b_a[l][None]
        gpre = pre_norm_g[l][None]
        lams = (diff_lambda_q1[l][None], diff_lambda_k1[l][None], diff_lambda_q2[l][None], diff_lambda_k2[l][None])
        merge_w = (gpre, w_gate, gla_norm_g[l][None], diff_subln_g[l][None], w_proj_gla[l].astype(BF16),
                   w_proj_diff[l].astype(BF16), w_out[l].astype(BF16), post_norm_g[l][None])

        k, v, q16, k16, v16 = _qkv_call(hp, gpre, w_qkv, _rope_tables(jnp.arange(T, dtype=jnp.int32)),
                                        _tile(T, 512))
        o_gla, s_p = _gla_call(hp, gpre, w_gla, wa, ba, zero_state, 0, _tile(T, 512), _tile(T, 128))
        tq = _tile(T, 512)
        bias = _diag_bias(tq)
        o_diff = None
        for n in range(T // tq, 0, -1):
            o_diff = _attn_call(lams, q16, k16, v16, bias, o_diff, n, tq, lambda_init)
        hp = _merge_call(hp.reshape(B * T, D_MODEL), o_gla.reshape(B * T, GLA_VAL),
                         o_diff.reshape(B * T, DIFF_VAL), *merge_w, _tile(B * T, 512),
                         lambda_init).reshape(B, T, D_MODEL)
        outs[0].append(k.reshape(B, T, DIFF_HEADS, DIFF_VD))
        outs[1].append(v.reshape(B, T, DIFF_HEADS, DIFF_VD))
        outs[2].append(s_p)

        pos_s = past_len + jnp.arange(Ts, dtype=jnp.int32)
        k, v, q16, k16, v16 = _qkv_call(hs, gpre, w_qkv, _rope_tables(pos_s), Ts)
        o_gla, s_s = _gla_call(hs, gpre, w_gla, wa, ba, state_in, l, Ts, Ts)
        o_diff = _attn_cache_call(lams, q16, cache_k, cache_v, k16, v16, l, lambda_init)
        hs = _merge_call(hs.reshape(Bs * Ts, D_MODEL), o_gla.reshape(Bs * Ts, GLA_VAL),
                         o_diff.reshape(Bs * Ts, DIFF_VAL), *merge_w, _tile(Bs * Ts, 256),
                         lambda_init).reshape(Bs, Ts, D_MODEL)
        outs[3].append(k.reshape(Bs, Ts, DIFF_HEADS, DIFF_VD))
        outs[4].append(v.reshape(Bs, Ts, DIFF_HEADS, DIFF_VD))
        outs[5].append(s_s)

    return (hp, hs, jnp.stack(outs[0]), jnp.stack(outs[1]), jnp.stack(outs[2]),
            jnp.stack(outs[3]), jnp.stack(outs[4]), jnp.stack(outs[5]))
```

```python
import functools
import math

import jax
import jax.numpy as jnp
from jax import lax
from jax.experimental import pallas as pl
from jax.experimental.pallas import tpu as pltpu

F32 = jnp.float32
BF16 = jnp.bfloat16

D_MODEL = 1024
EPS = 1e-6
CHUNK = 64

GLA_HEADS = 4
GLA_DK = 128
GLA_DV = 256
GLA_KEY = GLA_HEADS * GLA_DK
GLA_VAL = GLA_HEADS * GLA_DV
GLA_RANK = 16
GLA_TAU = 16.0
GLA_SCALE = GLA_DK ** -0.5

DIFF_HEADS = 8
DIFF_HD = 64
DIFF_VD = 128
DIFF_QK = DIFF_HEADS * 2 * DIFF_HD
DIFF_VAL = DIFF_HEADS * DIFF_VD
ROT_DIM = DIFF_HD // 4
ROPE_THETA = 500000.0
QK_SCALE = DIFF_HD ** -0.5
Q_PRESCALE = QK_SCALE * math.log2(math.e)

IN_SIZES = (GLA_KEY, GLA_KEY, GLA_VAL, GLA_RANK, GLA_VAL,
            DIFF_QK, DIFF_QK, DIFF_VAL, DIFF_VAL, D_MODEL, D_MODEL)

LANES = 128
VMEM_LIMIT = 56 * 1024 * 1024

_NT = (((1,), (1,)), ((), ()))
_TN = (((0,), (0,)), ((), ()))


def _rms_rows(x, g):
    return x * lax.rsqrt(jnp.mean(x * x, axis=-1, keepdims=True) + EPS) * g


def _params(n_axes):
    return pltpu.CompilerParams(dimension_semantics=("arbitrary",) * n_axes,
                                vmem_limit_bytes=VMEM_LIMIT)


def _const_spec(shape):
    zeros = (0,) * len(shape)
    return pl.BlockSpec(shape, lambda *_: zeros, pipeline_mode=pl.Buffered(1))


def _qkv_body(x_ref, g_ref, w_ref, c_ref, s1_ref, s2_ref, k_ref, v_ref, q16_ref, k16_ref, v16_ref):
    xn = _rms_rows(x_ref[0], g_ref[...]).astype(BF16)
    c, s1, s2 = c_ref[...], s1_ref[...], s2_ref[...]

    def rope(t):
        return t * c + pltpu.roll(t, LANES - ROT_DIM // 2, 1) * s1 + pltpu.roll(t, ROT_DIM // 2, 1) * s2

    q = jnp.dot(xn, w_ref[:, 0:DIFF_QK], preferred_element_type=F32)
    for h in range(DIFF_HEADS):
        sl = slice(h * LANES, (h + 1) * LANES)
        q16_ref[0, h] = (rope(q[:, sl]) * Q_PRESCALE).astype(BF16)
    k = jnp.dot(xn, w_ref[:, DIFF_QK:2 * DIFF_QK], preferred_element_type=F32)
    for h in range(DIFF_HEADS):
        sl = slice(h * LANES, (h + 1) * LANES)
        kr = rope(k[:, sl])
        k_ref[0, :, sl] = kr
        k16_ref[0, h] = kr.astype(BF16)
    v = jnp.dot(xn, w_ref[:, 2 * DIFF_QK:], preferred_element_type=F32)
    v_ref[0] = v
    for h in range(DIFF_HEADS):
        v16_ref[0, h] = v[:, h * LANES:(h + 1) * LANES].astype(BF16)


def _qkv_call(x, g, w, tabs, tm):
    B, T, _ = x.shape
    row = pl.BlockSpec((1, tm, D_MODEL), lambda b, i: (b, i, 0))
    heads = pl.BlockSpec((1, DIFF_HEADS, tm, LANES), lambda b, i: (b, 0, i, 0))
    tab = pl.BlockSpec((tm, LANES), lambda b, i: (i, 0))
    head_major = jax.ShapeDtypeStruct((B, DIFF_HEADS, T, LANES), BF16)
    return pl.pallas_call(
        _qkv_body,
        grid=(B, T // tm),
        in_specs=[row, _const_spec((1, D_MODEL)), _const_spec(w.shape), tab, tab, tab],
        out_specs=[row, row, heads, heads, heads],
        out_shape=[jax.ShapeDtypeStruct((B, T, DIFF_QK), F32),
                   jax.ShapeDtypeStruct((B, T, DIFF_VAL), F32),
                   head_major, head_major, head_major],
        compiler_params=_params(2),
        name="qkv_proj",
    )(x, g, w, *tabs)


def _rope_tables(pos):
    half = ROT_DIM // 2
    lane = jnp.arange(LANES, dtype=jnp.int32) % DIFF_HD
    inv = ROPE_THETA ** (-(2 * (lane % half)).astype(F32) / ROT_DIM)
    inv = jnp.where(lane < ROT_DIM, inv, 0.0)
    ang = pos.astype(F32)[:, None] * inv[None, :]
    cos, sin = jnp.cos(ang), jnp.sin(ang)
    s1 = jnp.where(lane[None, :] < half, -sin, 0.0)
    s2 = jnp.where((lane[None, :] >= half) & (lane[None, :] < ROT_DIM), sin, 0.0)
    return cos, s1, s2


def _gla_body(x_ref, g_ref, w_ref, wa_ref, ba_ref, s0_ref, o_ref, s_ref, st_ref, *, chunk):
    t = pl.program_id(1)
    C = chunk

    @pl.when(t == 0)
    def _():
        for h in range(GLA_HEADS):
            st_ref[h] = s0_ref[0, h].T

    ri = lax.broadcasted_iota(jnp.int32, (C, C), 0)
    ci = lax.broadcasted_iota(jnp.int32, (C, C), 1)
    tril = ri >= ci
    tril_b = jnp.where(tril, 1.0, 0.0).astype(BF16)

    heads = range(GLA_HEADS)
    ks = [slice(h * GLA_DK, (h + 1) * GLA_DK) for h in heads]
    vs = [slice(h * GLA_DV, (h + 1) * GLA_DV) for h in heads]

    n_chunks = x_ref.shape[1] // C
    xn = _rms_rows(x_ref[0], g_ref[...]).astype(BF16)
    u_all = jnp.dot(xn, w_ref[...], preferred_element_type=F32)
    z = jnp.dot(u_all[:, 2 * GLA_KEY + GLA_VAL:].astype(BF16), wa_ref[...],
                preferred_element_type=F32) + ba_ref[...]
    la_all = (jnp.minimum(z, 0.0) - jnp.log(1.0 + jnp.exp(-jnp.abs(z)))) * (1.0 / GLA_TAU)

    def operands(c):
        u, la = u_all[c * C:(c + 1) * C], la_all[c * C:(c + 1) * C]
        hi = la.astype(BF16)
        lo = (la - hi.astype(F32)).astype(BF16)
        b = (jnp.dot(tril_b, hi, preferred_element_type=F32)
             + jnp.dot(tril_b, lo, preferred_element_type=F32))
        r = b[C // 2 - 1:C // 2]
        bl = b[C - 1:C]
        q1 = u[:, 0:GLA_KEY] * GLA_SCALE * jnp.exp(b - r)
        qs = (q1 * jnp.exp(r)).astype(BF16)
        k1 = u[:, GLA_KEY:2 * GLA_KEY] * jnp.exp(r - b)
        kd = (k1 * jnp.exp(bl - r)).astype(BF16)
        v = u[:, 2 * GLA_KEY:2 * GLA_KEY + GLA_VAL].astype(BF16)
        return q1.astype(BF16), qs, k1.astype(BF16), kd, v, jnp.exp(bl)

    nxt = operands(0)
    for c in range(n_chunks):
        q1, qs, k1, kd, v, dec = nxt
        rows = slice(c * C, (c + 1) * C)
        a_raw = [lax.dot_general(q1[:, ks[h]], k1[:, ks[h]], _NT, preferred_element_type=F32) for h in heads]
        st = [st_ref[h] for h in heads]
        o_inter = [lax.dot_general(qs[:, ks[h]], st[h].astype(BF16), _NT, preferred_element_type=F32)
                   for h in heads]
        kv = [lax.dot_general(v[:, vs[h]], kd[:, ks[h]], _TN, preferred_element_type=F32) for h in heads]
        if c + 1 < n_chunks:
            nxt = operands(c + 1)
        for h in heads:
            a = jnp.where(tril, a_raw[h], 0.0).astype(BF16)
            o_ref[0, rows, vs[h]] = jnp.dot(a, v[:, vs[h]], preferred_element_type=F32) + o_inter[h]
            st_ref[h] = st[h] * dec[:, ks[h]] + kv[h]

    @pl.when(t == pl.num_programs(1) - 1)
    def _():
        for h in range(GLA_HEADS):
            s_ref[0, h] = st_ref[h].T


def _gla_call(x, g, w, wa, ba, s0, layer, tt, chunk):
    B, T, _ = x.shape
    row = pl.BlockSpec((1, tt, D_MODEL), lambda b, i: (b, i, 0))
    state_shape = (1, GLA_HEADS, GLA_DK, GLA_DV)
    return pl.pallas_call(
        functools.partial(_gla_body, chunk=chunk),
        grid=(B, T // tt),
        in_specs=[row, _const_spec((1, D_MODEL)), _const_spec(w.shape), _const_spec(wa.shape),
                  _const_spec((1, GLA_KEY)), pl.BlockSpec(state_shape, lambda b, i: (layer * B + b, 0, 0, 0))],
        out_specs=[row, pl.BlockSpec(state_shape, lambda b, i: (b, 0, 0, 0))],
        out_shape=[jax.ShapeDtypeStruct((B, T, GLA_VAL), F32),
                   jax.ShapeDtypeStruct((B, GLA_HEADS, GLA_DK, GLA_DV), F32)],
        scratch_shapes=[pltpu.VMEM((GLA_HEADS, GLA_DV, GLA_DK), F32)],
        compiler_params=_params(2),
        name="gla",
    )(x, g, w, wa, ba, s0)


def _lam(lq1, lk1, lq2, lk2, lambda_init):
    return (jnp.exp(jnp.sum(lq1[...] * lk1[...], axis=-1, keepdims=True))
            - jnp.exp(jnp.sum(lq2[...] * lk2[...], axis=-1, keepdims=True)) + lambda_init)


def _split_maps(q):
    lane = lax.broadcasted_iota(jnp.int32, q.shape, 1)
    zero = jnp.zeros_like(q)
    return jnp.where(lane < DIFF_HD, q, zero), jnp.where(lane >= DIFF_HD, q, zero)


def _ones_column(rows):
    lane = lax.broadcasted_iota(jnp.int32, (rows, LANES), 1)
    return jnp.where(lane == 0, 1.0, 0.0).astype(BF16)


def _attn_body(lq1, lk1, lq2, lk2, q_ref, k_ref, v_ref, bias_ref, *rest, n, tq, lambda_init):
    o_ref, s_ref = rest[-2:]
    heads = q_ref.shape[1]
    nblk = tq // LANES
    hq = tq // 2

    def lane_fold(s):
        return functools.reduce(jnp.maximum, [s[:, c * LANES:(c + 1) * LANES] for c in range(s.shape[1] // LANES)])

    mx = {}
    for g in range(heads):
        qz = _split_maps(q_ref[0, g])
        for j in range(n):
            kt = k_ref[0, g, j * tq:(j + 1) * tq, :]
            for m in range(2):
                if j < n - 1:
                    s = lax.dot_general(qz[m], kt, _NT, preferred_element_type=F32)
                    s_ref[g, m, j] = s
                    fold = lane_fold(s)
                else:
                    top = (lax.dot_general(qz[m][:hq], kt[:hq], _NT, preferred_element_type=F32)
                           + bias_ref[:hq, :hq])
                    bot = lax.dot_general(qz[m][hq:], kt, _NT, preferred_element_type=F32) + bias_ref[hq:, :]
                    s_ref[g, m, j, :hq, :hq] = top
                    s_ref[g, m, j, hq:, :] = bot
                    fold = jnp.concatenate([lane_fold(top), lane_fold(bot)], axis=0)
                mx[g, m] = fold if j == 0 else jnp.maximum(mx[g, m], fold)
    ones = _ones_column(tq)
    lam = _lam(lq1, lk1, lq2, lk2, lambda_init)
    for g in range(heads):
        acc = []
        for m in range(2):
            mrow = jnp.broadcast_to(jnp.max(mx[g, m], axis=-1, keepdims=True), (tq, LANES))
            mrow = jnp.concatenate([mrow] * nblk, axis=1)
            a = None
            for j in range(n):
                vt = jnp.concatenate([v_ref[0, g, j * tq:(j + 1) * tq, :], ones], axis=1)
                if j < n - 1:
                    p = jnp.exp2(s_ref[g, m, j] - mrow).astype(BF16)
                    d = jnp.dot(p, vt, preferred_element_type=F32)
                else:
                    p_top = jnp.exp2(s_ref[g, m, j, :hq, :hq] - mrow[:hq, :hq]).astype(BF16)
                    p_bot = jnp.exp2(s_ref[g, m, j, hq:, :] - mrow[hq:]).astype(BF16)
                    d = jnp.concatenate([jnp.dot(p_top, vt[:hq], preferred_element_type=F32),
                                         jnp.dot(p_bot, vt, preferred_element_type=F32)], axis=0)
                a = d if a is None else a + d
            acc.append(a)
        o = (acc[0][:, 0:DIFF_VD] / acc[0][:, DIFF_VD:DIFF_VD + 1]
             - lam * (acc[1][:, 0:DIFF_VD] / acc[1][:, DIFF_VD:DIFF_VD + 1])).astype(o_ref.dtype)
        cols = slice(g * DIFF_VD, (g + 1) * DIFF_VD)
        if o_ref.shape[1] == tq:
            o_ref[0, :, cols] = o
        else:
            o_ref[0, (n - 1) * tq:n * tq, cols] = o
            o_ref[0, 0:(n - 1) * tq, cols] = jnp.zeros(((n - 1) * tq, DIFF_VD), o_ref.dtype)


ATTN_SCORE_BYTES = 32 * 1024 * 1024


def _attn_call(lams, q, k, v, bias, o_prev, n, tq, lambda_init):
    B, _, T, _ = q.shape
    assert tq % (2 * LANES) == 0 and (tq // 2) % CHUNK == 0, tq
    heads = DIFF_HEADS
    while heads > 1 and heads * 2 * n * tq * tq * 4 > ATTN_SCORE_BYTES:
        heads //= 2
    lam_spec = _const_spec((1, DIFF_HD))
    qs = pl.BlockSpec((1, heads, tq, LANES), lambda b, h: (b, h, n - 1, 0))
    kv = pl.BlockSpec((1, heads, n * tq, LANES), lambda b, h: (b, h, 0, 0))
    in_specs = [lam_spec] * 4 + [qs, kv, kv, _const_spec((tq, tq))]
    args = (*lams, q, k, v, bias)
    if o_prev is None:
        assert n * tq == T, (n, tq, T)
        aliases = {}
        out_spec = pl.BlockSpec((1, T, heads * DIFF_VD), lambda b, h: (b, 0, h))
    else:
        in_specs.append(pl.BlockSpec(memory_space=pl.ANY))
        aliases = {len(args): 0}
        args = args + (o_prev,)
        out_spec = pl.BlockSpec((1, tq, heads * DIFF_VD), lambda b, h: (b, n - 1, h))
    return pl.pallas_call(
        functools.partial(_attn_body, n=n, tq=tq, lambda_init=lambda_init),
        grid=(B, DIFF_HEADS // heads),
        in_specs=in_specs,
        out_specs=out_spec,
        out_shape=jax.ShapeDtypeStruct((B, T, DIFF_VAL), BF16),
        scratch_shapes=[pltpu.VMEM((heads, 2, n, tq, tq), F32)],
        input_output_aliases=aliases,
        compiler_params=_params(2),
        name=f"diff_attn_q{n - 1}",
    )(*args)


def _diag_bias(tq):
    r = jnp.arange(tq, dtype=jnp.int32) // CHUNK
    return jnp.where(r[:, None] >= r[None, :], 0.0, -jnp.inf).astype(F32)


def _attn_cache_body(lq1, lk1, lq2, lk2, q_ref, kc_ref, vc_ref, kn_ref, vn_ref, o_ref, *, lambda_init):
    T = q_ref.shape[2]
    P = kc_ref.shape[1] // DIFF_HEADS
    lam = _lam(lq1, lk1, lq2, lk2, lambda_init)
    for h in range(DIFF_HEADS):
        qz = jnp.concatenate(_split_maps(q_ref[0, h]), axis=0)
        kc = kc_ref[0, pl.ds(h, P, stride=DIFF_HEADS), :].astype(BF16)
        vc = vc_ref[0, pl.ds(h, P, stride=DIFF_HEADS), :].astype(BF16)
        sc = lax.dot_general(qz, kc, _NT, preferred_element_type=F32)
        sn = lax.dot_general(qz, kn_ref[0, h], _NT, preferred_element_type=F32)
        mx = jnp.maximum(jnp.max(sc, axis=-1, keepdims=True), jnp.max(sn, axis=-1, keepdims=True))
        pc = jnp.exp2(sc - mx)
        pn = jnp.exp2(sn - mx)
        l = jnp.sum(pc, axis=-1, keepdims=True) + jnp.sum(pn, axis=-1, keepdims=True)
        o = (jnp.dot(pc.astype(BF16), vc, preferred_element_type=F32)
             + jnp.dot(pn.astype(BF16), vn_ref[0, h], preferred_element_type=F32)) / l
        o_ref[0, :, h * DIFF_VD:(h + 1) * DIFF_VD] = o[0:T] - lam * o[T:2 * T]


def _attn_cache_call(lams, q, kc, vc, kn, vn, layer, lambda_init):
    B, _, T, _ = q.shape
    lam_spec = _const_spec((1, DIFF_HD))
    new = pl.BlockSpec((1, DIFF_HEADS, T, LANES), lambda b: (b, 0, 0, 0))
    past = pl.BlockSpec((1, kc.shape[1], LANES), lambda b: (layer * B + b, 0, 0))
    return pl.pallas_call(
        functools.partial(_attn_cache_body, lambda_init=lambda_init),
        grid=(B,),
        in_specs=[lam_spec] * 4 + [new, past, past, new, new],
        out_specs=pl.BlockSpec((1, T, DIFF_VAL), lambda b: (b, 0, 0)),
        out_shape=jax.ShapeDtypeStruct((B, T, DIFF_VAL), F32),
        compiler_params=_params(1),
        name="diff_attn_cache",
    )(*lams, q, kc, vc, kn, vn)


MERGE_ROWS = 256


def _merge_body(x_ref, og_ref, od_ref, gpre_ref, wg_ref, ggla_ref, gsub_ref, wpg_ref, wpd_ref, wo_ref,
                gpost_ref, y_ref, yg_s, yd_s, *, lambda_init):
    tm = x_ref.shape[0]
    blocks = [slice(r, r + min(MERGE_ROWS, tm)) for r in range(0, tm, min(MERGE_ROWS, tm))]
    gates = [jnp.dot(_rms_rows(x_ref[rows], gpre_ref[...]).astype(BF16), wg_ref[...],
                     preferred_element_type=F32) for rows in blocks]
    for rows, gt in zip(blocks, gates):
        og, od = og_ref[rows], od_ref[rows].astype(F32)
        for h in range(GLA_HEADS):
            sl = slice(h * GLA_DV, (h + 1) * GLA_DV)
            yg_s[rows, sl] = (_rms_rows(og[:, sl], ggla_ref[...]) * jax.nn.silu(gt[:, sl])).astype(BF16)
        for h in range(DIFF_HEADS):
            sl = slice(h * DIFF_VD, (h + 1) * DIFF_VD)
            dg = gt[:, GLA_VAL + h * DIFF_VD:GLA_VAL + (h + 1) * DIFF_VD]
            yd_s[rows, sl] = (_rms_rows(od[:, sl], gsub_ref[...]) * (1.0 - lambda_init)
                              * jax.nn.silu(dg)).astype(BF16)
    merged = []
    for rows, gt in zip(blocks, gates):
        ma = gt[:, GLA_VAL + DIFF_VAL:GLA_VAL + DIFF_VAL + D_MODEL]
        mb = gt[:, GLA_VAL + DIFF_VAL + D_MODEL:]
        merged.append(jax.nn.sigmoid(ma) * jnp.dot(yg_s[rows], wpg_ref[...], preferred_element_type=F32)
                      + jax.nn.sigmoid(mb) * jnp.dot(yd_s[rows], wpd_ref[...], preferred_element_type=F32))
    for rows, mg in zip(blocks, merged):
        z = jnp.dot(mg.astype(BF16), wo_ref[...], preferred_element_type=F32)
        y_ref[rows] = x_ref[rows] + _rms_rows(z, gpost_ref[...])


def _merge_call(x, og, od, gpre, wg, ggla, gsub, wpg, wpd, wo, gpost, tm, lambda_init):
    N = x.shape[0]
    row = pl.BlockSpec((tm, D_MODEL), lambda i: (i, 0))
    return pl.pallas_call(
        functools.partial(_merge_body, lambda_init=lambda_init),
        grid=(N // tm,),
        in_specs=[row, row, row, _const_spec((1, D_MODEL)), _const_spec(wg.shape), _const_spec((1, GLA_DV)),
                  _const_spec((1, DIFF_VD)), _const_spec(wpg.shape), _const_spec(wpd.shape),
                  _const_spec(wo.shape), _const_spec((1, D_MODEL))],
        out_specs=row,
        out_shape=jax.ShapeDtypeStruct((N, D_MODEL), F32),
        scratch_shapes=[pltpu.VMEM((tm, GLA_VAL), BF16), pltpu.VMEM((tm, DIFF_VAL), BF16)],
        compiler_params=_params(1),
        name="merge",
    )(x, og, od, gpre, wg, ggla, gsub, wpg, wpd, wo, gpost)


def _tile(n, target):
    t = min(n, target)
    while n % t:
        t -= 1
    return t


def kernel(x_prompt, x_sample, cache_diff_k, cache_diff_v, state_gla, pre_norm_g, w_in, gla_w_a2, gla_b_a,
           gla_norm_g, diff_lambda_q1, diff_lambda_k1, diff_lambda_q2, diff_lambda_k2, diff_subln_g,
           w_proj_gla, w_proj_diff, w_out, post_norm_g):
    depth = w_in.shape[0]
    past_len = cache_diff_k.shape[2]
    hp, hs = x_prompt, x_sample
    B, T, _ = hp.shape
    Bs, Ts, _ = hs.shape
    cache_k = cache_diff_k.reshape(depth * Bs, past_len * DIFF_HEADS, 2 * DIFF_HD)
    cache_v = cache_diff_v.reshape(depth * Bs, past_len * DIFF_HEADS, DIFF_VD)
    state_in = state_gla.astype(F32).reshape(depth * Bs, GLA_HEADS, GLA_DK, GLA_DV)
    zero_state = jnp.zeros((B, GLA_HEADS, GLA_DK, GLA_DV), F32)
    outs = [[] for _ in range(6)]
    offs = [0]
    for s in IN_SIZES:
        offs.append(offs[-1] + s)
    for l in range(depth):
        lambda_init = 0.8 - 0.6 * math.exp(-0.3 * l)
        def cols(first, last):
            return w_in[l, :, offs[first]:offs[last + 1]].astype(BF16)

        w_gla = jnp.concatenate([cols(0, 2), jnp.pad(cols(3, 3), ((0, 0), (0, LANES - GLA_RANK)))], axis=1)
        w_qkv = cols(5, 7)
        w_gate = jnp.concatenate([cols(4, 4), cols(8, 10)], axis=1)
        wa = jnp.pad(gla_w_a2[l], ((0, LANES - GLA_RANK), (0, 0))).astype(BF16)
        ba = gla_b_a[l][None]
        gpre = pre_norm_g[l][None]
        lams = (diff_lambda_q1[l][None], diff_lambda_k1[l][None], diff_lambda_q2[l][None], diff_lambda_k2[l][None])
        merge_w = (gpre, w_gate, gla_norm_g[l][None], diff_subln_g[l][None], w_proj_gla[l].astype(BF16),
                   w_proj_diff[l].astype(BF16), w_out[l].astype(BF16), post_norm_g[l][None])

        k, v, q16, k16, v16 = _qkv_call(hp, gpre, w_qkv, _rope_tables(jnp.arange(T, dtype=jnp.int32)),
                                        _tile(T, 512))
        o_gla, s_p = _gla_call(hp, gpre, w_gla, wa, ba, zero_state, 0, _tile(T, 512), _tile(T, 128))
        tq = _tile(T, 512)
        bias = _diag_bias(tq)
        o_diff = None
        for n in range(T // tq, 0, -1):
            o_diff = _attn_call(lams, q16, k16, v16, bias, o_diff, n, tq, lambda_init)
        hp = _merge_call(hp.reshape(B * T, D_MODEL), o_gla.reshape(B * T, GLA_VAL),
                         o_diff.reshape(B * T, DIFF_VAL), *merge_w, _tile(B * T, 512),
                         lambda_init).reshape(B, T, D_MODEL)
        outs[0].append(k.reshape(B, T, DIFF_HEADS, DIFF_VD))
        outs[1].append(v.reshape(B, T, DIFF_HEADS, DIFF_VD))
        outs[2].append(s_p)

        pos_s = past_len + jnp.arange(Ts, dtype=jnp.int32)
        k, v, q16, k16, v16 = _qkv_call(hs, gpre, w_qkv, _rope_tables(pos_s), Ts)
        o_gla, s_s = _gla_call(hs, gpre, w_gla, wa, ba, state_in, l, Ts, Ts)
        o_diff = _attn_cache_call(lams, q16, cache_k, cache_v, k16, v16, l, lambda_init)
        hs = _merge_call(hs.reshape(Bs * Ts, D_MODEL), o_gla.reshape(Bs * Ts, GLA_VAL),
                         o_diff.reshape(Bs * Ts, DIFF_VAL), *merge_w, _tile(Bs * Ts, 256),
                         lambda_init).reshape(Bs, Ts, D_MODEL)
        outs[3].append(k.reshape(Bs, Ts, DIFF_HEADS, DIFF_VD))
        outs[4].append(v.reshape(Bs, Ts, DIFF_HEADS, DIFF_VD))
        outs[5].append(s_s)

    return (hp, hs, jnp.stack(outs[0]), jnp.stack(outs[1]), jnp.stack(outs[2]),
            jnp.stack(outs[3]), jnp.stack(outs[4]), jnp.stack(outs[5]))
```

```python
import functools
import math

import jax
import jax.numpy as jnp
from jax import lax
from jax.experimental import pallas as pl
from jax.experimental.pallas import tpu as pltpu

F32 = jnp.float32
BF16 = jnp.bfloat16

D_MODEL = 1024
EPS = 1e-6
CHUNK = 64

GLA_HEADS = 4
GLA_DK = 128
GLA_DV = 256
GLA_KEY = GLA_HEADS * GLA_DK
GLA_VAL = GLA_HEADS * GLA_DV
GLA_RANK = 16
GLA_TAU = 16.0
GLA_SCALE = GLA_DK ** -0.5

DIFF_HEADS = 8
DIFF_HD = 64
DIFF_VD = 128
DIFF_QK = DIFF_HEADS * 2 * DIFF_HD
DIFF_VAL = DIFF_HEADS * DIFF_VD
ROT_DIM = DIFF_HD // 4
ROPE_THETA = 500000.0
QK_SCALE = DIFF_HD ** -0.5
Q_PRESCALE = QK_SCALE * math.log2(math.e)

IN_SIZES = (GLA_KEY, GLA_KEY, GLA_VAL, GLA_RANK, GLA_VAL,
            DIFF_QK, DIFF_QK, DIFF_VAL, DIFF_VAL, D_MODEL, D_MODEL)

LANES = 128
VMEM_LIMIT = 56 * 1024 * 1024

_NT = (((1,), (1,)), ((), ()))
_TN = (((0,), (0,)), ((), ()))


def _rms_rows(x, g):
    return x * lax.rsqrt(jnp.mean(x * x, axis=-1, keepdims=True) + EPS) * g


def _params(n_axes):
    return pltpu.CompilerParams(dimension_semantics=("arbitrary",) * n_axes,
                                vmem_limit_bytes=VMEM_LIMIT)


def _const_spec(shape):
    zeros = (0,) * len(shape)
    return pl.BlockSpec(shape, lambda *_: zeros, pipeline_mode=pl.Buffered(1))


def _qkv_body(x_ref, g_ref, w_ref, c_ref, s1_ref, s2_ref, k_ref, v_ref, q16_ref, k16_ref, v16_ref):
    xn = _rms_rows(x_ref[0], g_ref[...]).astype(BF16)
    c, s1, s2 = c_ref[...], s1_ref[...], s2_ref[...]

    def rope(t):
        return t * c + pltpu.roll(t, LANES - ROT_DIM // 2, 1) * s1 + pltpu.roll(t, ROT_DIM // 2, 1) * s2

    q = jnp.dot(xn, w_ref[:, 0:DIFF_QK], preferred_element_type=F32)
    for h in range(DIFF_HEADS):
        sl = slice(h * LANES, (h + 1) * LANES)
        q16_ref[0, h] = (rope(q[:, sl]) * Q_PRESCALE).astype(BF16)
    k = jnp.dot(xn, w_ref[:, DIFF_QK:2 * DIFF_QK], preferred_element_type=F32)
    for h in range(DIFF_HEADS):
        sl = slice(h * LANES, (h + 1) * LANES)
        kr = rope(k[:, sl])
        k_ref[0, :, sl] = kr
        k16_ref[0, h] = kr.astype(BF16)
    v = jnp.dot(xn, w_ref[:, 2 * DIFF_QK:], preferred_element_type=F32)
    v_ref[0] = v
    for h in range(DIFF_HEADS):
        v16_ref[0, h] = v[:, h * LANES:(h + 1) * LANES].astype(BF16)


def _qkv_call(x, g, w, tabs, tm):
    B, T, _ = x.shape
    row = pl.BlockSpec((1, tm, D_MODEL), lambda b, i: (b, i, 0))
    heads = pl.BlockSpec((1, DIFF_HEADS, tm, LANES), lambda b, i: (b, 0, i, 0))
    tab = pl.BlockSpec((tm, LANES), lambda b, i: (i, 0))
    head_major = jax.ShapeDtypeStruct((B, DIFF_HEADS, T, LANES), BF16)
    return pl.pallas_call(
        _qkv_body,
        grid=(B, T // tm),
        in_specs=[row, _const_spec((1, D_MODEL)), _const_spec(w.shape), tab, tab, tab],
        out_specs=[row, row, heads, heads, heads],
        out_shape=[jax.ShapeDtypeStruct((B, T, DIFF_QK), F32),
                   jax.ShapeDtypeStruct((B, T, DIFF_VAL), F32),
                   head_major, head_major, head_major],
        compiler_params=_params(2),
        name="qkv_proj",
    )(x, g, w, *tabs)


def _rope_tables(pos):
    half = ROT_DIM // 2
    lane = jnp.arange(LANES, dtype=jnp.int32) % DIFF_HD
    inv = ROPE_THETA ** (-(2 * (lane % half)).astype(F32) / ROT_DIM)
    inv = jnp.where(lane < ROT_DIM, inv, 0.0)
    ang = pos.astype(F32)[:, None] * inv[None, :]
    cos, sin = jnp.cos(ang), jnp.sin(ang)
    s1 = jnp.where(lane[None, :] < half, -sin, 0.0)
    s2 = jnp.where((lane[None, :] >= half) & (lane[None, :] < ROT_DIM), sin, 0.0)
    return cos, s1, s2


def _gla_body(x_ref, g_ref, w_ref, wa_ref, ba_ref, s0_ref, o_ref, s_ref, st_ref, *, chunk):
    t = pl.program_id(1)
    C = chunk

    @pl.when(t == 0)
    def _():
        for h in range(GLA_HEADS):
            st_ref[h] = s0_ref[0, h].T

    ri = lax.broadcasted_iota(jnp.int32, (C, C), 0)
    ci = lax.broadcasted_iota(jnp.int32, (C, C), 1)
    tril = ri >= ci
    tril_b = jnp.where(tril, 1.0, 0.0).astype(BF16)

    heads = range(GLA_HEADS)
    ks = [slice(h * GLA_DK, (h + 1) * GLA_DK) for h in heads]
    vs = [slice(h * GLA_DV, (h + 1) * GLA_DV) for h in heads]

    n_chunks = x_ref.shape[1] // C
    xn = _rms_rows(x_ref[0], g_ref[...]).astype(BF16)
    gr = jnp.dot(xn, w_ref[:, 2 * GLA_KEY + GLA_VAL:], preferred_element_type=F32)
    z = jnp.dot(gr.astype(BF16), wa_ref[...], preferred_element_type=F32) + ba_ref[...]
    la_all = (jnp.minimum(z, 0.0) - jnp.log(1.0 + jnp.exp(-jnp.abs(z)))) * (1.0 / GLA_TAU)
    u_all = jnp.dot(xn, w_ref[:, 0:2 * GLA_KEY + GLA_VAL], preferred_element_type=F32)

    def operands(c):
        u, la = u_all[c * C:(c + 1) * C], la_all[c * C:(c + 1) * C]
        hi = la.astype(BF16)
        lo = (la - hi.astype(F32)).astype(BF16)
        b = (jnp.dot(tril_b, hi, preferred_element_type=F32)
             + jnp.dot(tril_b, lo, preferred_element_type=F32))
        r = b[C // 2 - 1:C // 2]
        bl = b[C - 1:C]
        q1 = u[:, 0:GLA_KEY] * GLA_SCALE * jnp.exp(b - r)
        qs = (q1 * jnp.exp(r)).astype(BF16)
        k1 = u[:, GLA_KEY:2 * GLA_KEY] * jnp.exp(r - b)
        kd = (k1 * jnp.exp(bl - r)).astype(BF16)
        v = u[:, 2 * GLA_KEY:2 * GLA_KEY + GLA_VAL].astype(BF16)
        return q1.astype(BF16), qs, k1.astype(BF16), kd, v, jnp.exp(bl)

    ops = [operands(c) for c in range(n_chunks)]
    a_raw = [[lax.dot_general(q1[:, ks[h]], k1[:, ks[h]], _NT, preferred_element_type=F32) for h in heads]
             for (q1, qs, k1, kd, v, dec) in ops]
    kv = [[lax.dot_general(v[:, vs[h]], kd[:, ks[h]], _TN, preferred_element_type=F32) for h in heads]
          for (q1, qs, k1, kd, v, dec) in ops]
    intra = [[jnp.dot(jnp.where(tril, a_raw[c][h], 0.0).astype(BF16), ops[c][4][:, vs[h]],
                      preferred_element_type=F32) for h in heads] for c in range(n_chunks)]
    for c in range(n_chunks):
        q1, qs, k1, kd, v, dec = ops[c]
        rows = slice(c * C, (c + 1) * C)
        for h in heads:
            st = st_ref[h]
            o_ref[0, rows, vs[h]] = intra[c][h] + lax.dot_general(qs[:, ks[h]], st.astype(BF16), _NT,
                                                                  preferred_element_type=F32)
            st_ref[h] = st * dec[:, ks[h]] + kv[c][h]

    @pl.when(t == pl.num_programs(1) - 1)
    def _():
        for h in range(GLA_HEADS):
            s_ref[0, h] = st_ref[h].T


def _gla_call(x, g, w, wa, ba, s0, layer, tt, chunk):
    B, T, _ = x.shape
    row = pl.BlockSpec((1, tt, D_MODEL), lambda b, i: (b, i, 0))
    state_shape = (1, GLA_HEADS, GLA_DK, GLA_DV)
    return pl.pallas_call(
        functools.partial(_gla_body, chunk=chunk),
        grid=(B, T // tt),
        in_specs=[row, _const_spec((1, D_MODEL)), _const_spec(w.shape), _const_spec(wa.shape),
                  _const_spec((1, GLA_KEY)), pl.BlockSpec(state_shape, lambda b, i: (layer * B + b, 0, 0, 0))],
        out_specs=[row, pl.BlockSpec(state_shape, lambda b, i: (b, 0, 0, 0))],
        out_shape=[jax.ShapeDtypeStruct((B, T, GLA_VAL), F32),
                   jax.ShapeDtypeStruct((B, GLA_HEADS, GLA_DK, GLA_DV), F32)],
        scratch_shapes=[pltpu.VMEM((GLA_HEADS, GLA_DV, GLA_DK), F32)],
        compiler_params=_params(2),
        name="gla",
    )(x, g, w, wa, ba, s0)


def _lam(lq1, lk1, lq2, lk2, lambda_init):
    return (jnp.exp(jnp.sum(lq1[...] * lk1[...], axis=-1, keepdims=True))
            - jnp.exp(jnp.sum(lq2[...] * lk2[...], axis=-1, keepdims=True)) + lambda_init)


def _split_maps(q):
    lane = lax.broadcasted_iota(jnp.int32, q.shape, 1)
    zero = jnp.zeros_like(q)
    return jnp.where(lane < DIFF_HD, q, zero), jnp.where(lane >= DIFF_HD, q, zero)


def _ones_column(rows):
    lane = lax.broadcasted_iota(jnp.int32, (rows, LANES), 1)
    return jnp.where(lane == 0, 1.0, 0.0).astype(BF16)


def _attn_body(lq1, lk1, lq2, lk2, q_ref, k_ref, v_ref, bias_ref, *rest, n, tq, lambda_init):
    o_ref, s_ref = rest[-2:]
    heads = q_ref.shape[1]
    nblk = tq // LANES
    hq = tq // 2

    def lane_fold(s):
        return functools.reduce(jnp.maximum, [s[:, c * LANES:(c + 1) * LANES] for c in range(s.shape[1] // LANES)])

    mx = {}
    for g in range(heads):
        qz = _split_maps(q_ref[0, g])
        for j in range(n):
            kt = k_ref[0, g, j * tq:(j + 1) * tq, :]
            for m in range(2):
                if j < n - 1:
                    s = lax.dot_general(qz[m], kt, _NT, preferred_element_type=F32)
                    s_ref[g, m, j] = s
                    fold = lane_fold(s)
                else:
                    top = (lax.dot_general(qz[m][:hq], kt[:hq], _NT, preferred_element_type=F32)
                           + bias_ref[:hq, :hq])
                    bot = lax.dot_general(qz[m][hq:], kt, _NT, preferred_element_type=F32) + bias_ref[hq:, :]
                    s_ref[g, m, j, :hq, :hq] = top
                    s_ref[g, m, j, hq:, :] = bot
                    fold = jnp.concatenate([lane_fold(top), lane_fold(bot)], axis=0)
                mx[g, m] = fold if j == 0 else jnp.maximum(mx[g, m], fold)
    ones = _ones_column(tq)
    lam = _lam(lq1, lk1, lq2, lk2, lambda_init)
    for g in range(heads):
        acc = []
        for m in range(2):
            mrow = jnp.broadcast_to(jnp.max(mx[g, m], axis=-1, keepdims=True), (tq, LANES))
            mrow = jnp.concatenate([mrow] * nblk, axis=1)
            a = None
            for j in range(n):
                vt = jnp.concatenate([v_ref[0, g, j * tq:(j + 1) * tq, :], ones], axis=1)
                if j < n - 1:
                    p = jnp.exp2(s_ref[g, m, j] - mrow).astype(BF16)
                    d = jnp.dot(p, vt, preferred_element_type=F32)
                else:
                    p_top = jnp.exp2(s_ref[g, m, j, :hq, :hq] - mrow[:hq, :hq]).astype(BF16)
                    p_bot = jnp.exp2(s_ref[g, m, j, hq:, :] - mrow[hq:]).astype(BF16)
                    d = jnp.concatenate([jnp.dot(p_top, vt[:hq], preferred_element_type=F32),
                                         jnp.dot(p_bot, vt, preferred_element_type=F32)], axis=0)
                a = d if a is None else a + d
            acc.append(a)
        o = (acc[0][:, 0:DIFF_VD] / acc[0][:, DIFF_VD:DIFF_VD + 1]
             - lam * (acc[1][:, 0:DIFF_VD] / acc[1][:, DIFF_VD:DIFF_VD + 1])).astype(o_ref.dtype)
        cols = slice(g * DIFF_VD, (g + 1) * DIFF_VD)
        if o_ref.shape[1] == tq:
            o_ref[0, :, cols] = o
        else:
            o_ref[0, (n - 1) * tq:n * tq, cols] = o
            o_ref[0, 0:(n - 1) * tq, cols] = jnp.zeros(((n - 1) * tq, DIFF_VD), o_ref.dtype)


ATTN_SCORE_BYTES = 32 * 1024 * 1024


def _attn_call(lams, q, k, v, bias, o_prev, n, tq, lambda_init):
    B, _, T, _ = q.shape
    assert tq % (2 * LANES) == 0 and (tq // 2) % CHUNK == 0, tq
    heads = DIFF_HEADS
    while heads > 1 and heads * 2 * n * tq * tq * 4 > ATTN_SCORE_BYTES:
        heads //= 2
    lam_spec = _const_spec((1, DIFF_HD))
    qs = pl.BlockSpec((1, heads, tq, LANES), lambda b, h: (b, h, n - 1, 0))
    kv = pl.BlockSpec((1, heads, n * tq, LANES), lambda b, h: (b, h, 0, 0))
    in_specs = [lam_spec] * 4 + [qs, kv, kv, _const_spec((tq, tq))]
    args = (*lams, q, k, v, bias)
    if o_prev is None:
        assert n * tq == T, (n, tq, T)
        aliases = {}
        out_spec = pl.BlockSpec((1, T, heads * DIFF_VD), lambda b, h: (b, 0, h))
    else:
        in_specs.append(pl.BlockSpec(memory_space=pl.ANY))
        aliases = {len(args): 0}
        args = args + (o_prev,)
        out_spec = pl.BlockSpec((1, tq, heads * DIFF_VD), lambda b, h: (b, n - 1, h))
    return pl.pallas_call(
        functools.partial(_attn_body, n=n, tq=tq, lambda_init=lambda_init),
        grid=(B, DIFF_HEADS // heads),
        in_specs=in_specs,
        out_specs=out_spec,
        out_shape=jax.ShapeDtypeStruct((B, T, DIFF_VAL), BF16),
        scratch_shapes=[pltpu.VMEM((heads, 2, n, tq, tq), F32)],
        input_output_aliases=aliases,
        compiler_params=_params(2),
        name=f"diff_attn_q{n - 1}",
    )(*args)


def _diag_bias(tq):
    r = jnp.arange(tq, dtype=jnp.int32) // CHUNK
    return jnp.where(r[:, None] >= r[None, :], 0.0, -jnp.inf).astype(F32)


def _attn_cache_body(lq1, lk1, lq2, lk2, q_ref, kc_ref, vc_ref, kn_ref, vn_ref, o_ref, *, lambda_init):
    T = q_ref.shape[2]
    P = kc_ref.shape[1] // DIFF_HEADS
    lam = _lam(lq1, lk1, lq2, lk2, lambda_init)
    for h in range(DIFF_HEADS):
        qz = jnp.concatenate(_split_maps(q_ref[0, h]), axis=0)
        kc = kc_ref[0, pl.ds(h, P, stride=DIFF_HEADS), :].astype(BF16)
        vc = vc_ref[0, pl.ds(h, P, stride=DIFF_HEADS), :].astype(BF16)
        sc = lax.dot_general(qz, kc, _NT, preferred_element_type=F32)
        sn = lax.dot_general(qz, kn_ref[0, h], _NT, preferred_element_type=F32)
        mx = jnp.maximum(jnp.max(sc, axis=-1, keepdims=True), jnp.max(sn, axis=-1, keepdims=True))
        pc = jnp.exp2(sc - mx)
        pn = jnp.exp2(sn - mx)
        l = jnp.sum(pc, axis=-1, keepdims=True) + jnp.sum(pn, axis=-1, keepdims=True)
        o = (jnp.dot(pc.astype(BF16), vc, preferred_element_type=F32)
             + jnp.dot(pn.astype(BF16), vn_ref[0, h], preferred_element_type=F32)) / l
        o_ref[0, :, h * DIFF_VD:(h + 1) * DIFF_VD] = o[0:T] - lam * o[T:2 * T]


def _attn_cache_call(lams, q, kc, vc, kn, vn, layer, lambda_init):
    B, _, T, _ = q.shape
    lam_spec = _const_spec((1, DIFF_HD))
    new = pl.BlockSpec((1, DIFF_HEADS, T, LANES), lambda b: (b, 0, 0, 0))
    past = pl.BlockSpec((1, kc.shape[1], LANES), lambda b: (layer * B + b, 0, 0))
    return pl.pallas_call(
        functools.partial(_attn_cache_body, lambda_init=lambda_init),
        grid=(B,),
        in_specs=[lam_spec] * 4 + [new, past, past, new, new],
        out_specs=pl.BlockSpec((1, T, DIFF_VAL), lambda b: (b, 0, 0)),
        out_shape=jax.ShapeDtypeStruct((B, T, DIFF_VAL), F32),
        compiler_params=_params(1),
        name="diff_attn_cache",
    )(*lams, q, kc, vc, kn, vn)


MERGE_ROWS = 256


def _merge_body(x_ref, og_ref, od_ref, gpre_ref, wg_ref, ggla_ref, gsub_ref, wpg_ref, wpd_ref, wo_ref,
                gpost_ref, y_ref, yg_s, yd_s, *, lambda_init):
    tm = x_ref.shape[0]
    blocks = [slice(r, r + min(MERGE_ROWS, tm)) for r in range(0, tm, min(MERGE_ROWS, tm))]
    gates = [jnp.dot(_rms_rows(x_ref[rows], gpre_ref[...]).astype(BF16), wg_ref[...],
                     preferred_element_type=F32) for rows in blocks]
    for rows, gt in zip(blocks, gates):
        og, od = og_ref[rows], od_ref[rows].astype(F32)
        for h in range(GLA_HEADS):
            sl = slice(h * GLA_DV, (h + 1) * GLA_DV)
            yg_s[rows, sl] = (_rms_rows(og[:, sl], ggla_ref[...]) * jax.nn.silu(gt[:, sl])).astype(BF16)
        for h in range(DIFF_HEADS):
            sl = slice(h * DIFF_VD, (h + 1) * DIFF_VD)
            dg = gt[:, GLA_VAL + h * DIFF_VD:GLA_VAL + (h + 1) * DIFF_VD]
            yd_s[rows, sl] = (_rms_rows(od[:, sl], gsub_ref[...]) * (1.0 - lambda_init)
                              * jax.nn.silu(dg)).astype(BF16)
    merged = []
    for rows, gt in zip(blocks, gates):
        ma = gt[:, GLA_VAL + DIFF_VAL:GLA_VAL + DIFF_VAL + D_MODEL]
        mb = gt[:, GLA_VAL + DIFF_VAL + D_MODEL:]
        merged.append(jax.nn.sigmoid(ma) * jnp.dot(yg_s[rows], wpg_ref[...], preferred_element_type=F32)
                      + jax.nn.sigmoid(mb) * jnp.dot(yd_s[rows], wpd_ref[...], preferred_element_type=F32))
    for rows, mg in zip(blocks, merged):
        z = jnp.dot(mg.astype(BF16), wo_ref[...], preferred_element_type=F32)
        y_ref[rows] = x_ref[rows] + _rms_rows(z, gpost_ref[...])


def _merge_call(x, og, od, gpre, wg, ggla, gsub, wpg, wpd, wo, gpost, tm, lambda_init):
    N = x.shape[0]
    row = pl.BlockSpec((tm, D_MODEL), lambda i: (i, 0))
    return pl.pallas_call(
        functools.partial(_merge_body, lambda_init=lambda_init),
        grid=(N // tm,),
        in_specs=[row, row, row, _const_spec((1, D_MODEL)), _const_spec(wg.shape), _const_spec((1, GLA_DV)),
                  _const_spec((1, DIFF_VD)), _const_spec(wpg.shape), _const_spec(wpd.shape),
                  _const_spec(wo.shape), _const_spec((1, D_MODEL))],
        out_specs=row,
        out_shape=jax.ShapeDtypeStruct((N, D_MODEL), F32),
        scratch_shapes=[pltpu.VMEM((tm, GLA_VAL), BF16), pltpu.VMEM((tm, DIFF_VAL), BF16)],
        compiler_params=_params(1),
        name="merge",
    )(x, og, od, gpre, wg, ggla, gsub, wpg, wpd, wo, gpost)


def _tile(n, target):
    t = min(n, target)
    while n % t:
        t -= 1
    return t


def kernel(x_prompt, x_sample, cache_diff_k, cache_diff_v, state_gla, pre_norm_g, w_in, gla_w_a2, gla_b_a,
           gla_norm_g, diff_lambda_q1, diff_lambda_k1, diff_lambda_q2, diff_lambda_k2, diff_subln_g,
           w_proj_gla, w_proj_diff, w_out, post_norm_g):
    depth = w_in.shape[0]
    past_len = cache_diff_k.shape[2]
    hp, hs = x_prompt, x_sample
    B, T, _ = hp.shape
    Bs, Ts, _ = hs.shape
    cache_k = cache_diff_k.reshape(depth * Bs, past_len * DIFF_HEADS, 2 * DIFF_HD)
    cache_v = cache_diff_v.reshape(depth * Bs, past_len * DIFF_HEADS, DIFF_VD)
    state_in = state_gla.astype(F32).reshape(depth * Bs, GLA_HEADS, GLA_DK, GLA_DV)
    zero_state = jnp.zeros((B, GLA_HEADS, GLA_DK, GLA_DV), F32)
    outs = [[] for _ in range(6)]
    offs = [0]
    for s in IN_SIZES:
        offs.append(offs[-1] + s)
    for l in range(depth):
        lambda_init = 0.8 - 0.6 * math.exp(-0.3 * l)
        def cols(first, last):
            return w_in[l, :, offs[first]:offs[last + 1]].astype(BF16)

        w_gla = jnp.concatenate([cols(0, 2), jnp.pad(cols(3, 3), ((0, 0), (0, LANES - GLA_RANK)))], axis=1)
        w_qkv = cols(5, 7)
        w_gate = jnp.concatenate([cols(4, 4), cols(8, 10)], axis=1)
        wa = jnp.pad(gla_w_a2[l], ((0, LANES - GLA_RANK), (0, 0))).astype(BF16)
        ba = gla_b_a[l][None]
        gpre = pre_norm_g[l][None]
        lams = (diff_lambda_q1[l][None], diff_lambda_k1[l][None], diff_lambda_q2[l][None], diff_lambda_k2[l][None])
        merge_w = (gpre, w_gate, gla_norm_g[l][None], diff_subln_g[l][None], w_proj_gla[l].astype(BF16),
                   w_proj_diff[l].astype(BF16), w_out[l].astype(BF16), post_norm_g[l][None])

        k, v, q16, k16, v16 = _qkv_call(hp, gpre, w_qkv, _rope_tables(jnp.arange(T, dtype=jnp.int32)),
                                        _tile(T, 512))
        o_gla, s_p = _gla_call(hp, gpre, w_gla, wa, ba, zero_state, 0, _tile(T, 1024), _tile(T, 128))
        tq = _tile(T, 512)
        bias = _diag_bias(tq)
        o_diff = None
        for n in range(T // tq, 0, -1):
            o_diff = _attn_call(lams, q16, k16, v16, bias, o_diff, n, tq, lambda_init)
        hp = _merge_call(hp.reshape(B * T, D_MODEL), o_gla.reshape(B * T, GLA_VAL),
                         o_diff.reshape(B * T, DIFF_VAL), *merge_w, _tile(B * T, 512),
                         lambda_init).reshape(B, T, D_MODEL)
        outs[0].append(k.reshape(B, T, DIFF_HEADS, DIFF_VD))
        outs[1].append(v.reshape(B, T, DIFF_HEADS, DIFF_VD))
        outs[2].append(s_p)

        pos_s = past_len + jnp.arange(Ts, dtype=jnp.int32)
        k, v, q16, k16, v16 = _qkv_call(hs, gpre, w_qkv, _rope_tables(pos_s), Ts)
        o_gla, s_s = _gla_call(hs, gpre, w_gla, wa, ba, state_in, l, Ts, Ts)
        o_diff = _attn_cache_call(lams, q16, cache_k, cache_v, k16, v16, l, lambda_init)
        hs = _merge_call(hs.reshape(Bs * Ts, D_MODEL), o_gla.reshape(Bs * Ts, GLA_VAL),
                         o_diff.reshape(Bs * Ts, DIFF_VAL), *merge_w, _tile(Bs * Ts, 256),
                         lambda_init).reshape(Bs, Ts, D_MODEL)
        outs[3].append(k.reshape(Bs, Ts, DIFF_HEADS, DIFF_VD))
        outs[4].append(v.reshape(Bs, Ts, DIFF_HEADS, DIFF_VD))
        outs[5].append(s_s)

    return (hp, hs, jnp.stack(outs[0]), jnp.stack(outs[1]), jnp.stack(outs[2]),
            jnp.stack(outs[3]), jnp.stack(outs[4]), jnp.stack(outs[5]))
```

```python
import functools
import math

import jax
import jax.numpy as jnp
from jax import lax
from jax.experimental import pallas as pl
from jax.experimental.pallas import tpu as pltpu

F32 = jnp.float32
BF16 = jnp.bfloat16

D_MODEL = 1024
EPS = 1e-6
CHUNK = 64

GLA_HEADS = 4
GLA_DK = 128
GLA_DV = 256
GLA_KEY = GLA_HEADS * GLA_DK
GLA_VAL = GLA_HEADS * GLA_DV
GLA_RANK = 16
GLA_TAU = 16.0
GLA_SCALE = GLA_DK ** -0.5

DIFF_HEADS = 8
DIFF_HD = 64
DIFF_VD = 128
DIFF_QK = DIFF_HEADS * 2 * DIFF_HD
DIFF_VAL = DIFF_HEADS * DIFF_VD
ROT_DIM = DIFF_HD // 4
ROPE_THETA = 500000.0
QK_SCALE = DIFF_HD ** -0.5
Q_PRESCALE = QK_SCALE * math.log2(math.e)

IN_SIZES = (GLA_KEY, GLA_KEY, GLA_VAL, GLA_RANK, GLA_VAL,
            DIFF_QK, DIFF_QK, DIFF_VAL, DIFF_VAL, D_MODEL, D_MODEL)

LANES = 128
VMEM_LIMIT = 56 * 1024 * 1024

_NT = (((1,), (1,)), ((), ()))
_TN = (((0,), (0,)), ((), ()))


def _rms_rows(x, g):
    return x * lax.rsqrt(jnp.mean(x * x, axis=-1, keepdims=True) + EPS) * g


def _params(n_axes):
    return pltpu.CompilerParams(dimension_semantics=("arbitrary",) * n_axes,
                                vmem_limit_bytes=VMEM_LIMIT)


def _const_spec(shape):
    zeros = (0,) * len(shape)
    return pl.BlockSpec(shape, lambda *_: zeros, pipeline_mode=pl.Buffered(1))


def _qkv_body(x_ref, g_ref, w_ref, c_ref, s1_ref, s2_ref, k_ref, v_ref, q16_ref, k16_ref, v16_ref):
    xn = _rms_rows(x_ref[0], g_ref[...]).astype(BF16)
    c, s1, s2 = c_ref[...], s1_ref[...], s2_ref[...]

    def rope(t):
        return t * c + pltpu.roll(t, LANES - ROT_DIM // 2, 1) * s1 + pltpu.roll(t, ROT_DIM // 2, 1) * s2

    q = jnp.dot(xn, w_ref[:, 0:DIFF_QK], preferred_element_type=F32)
    for h in range(DIFF_HEADS):
        sl = slice(h * LANES, (h + 1) * LANES)
        q16_ref[0, h] = (rope(q[:, sl]) * Q_PRESCALE).astype(BF16)
    k = jnp.dot(xn, w_ref[:, DIFF_QK:2 * DIFF_QK], preferred_element_type=F32)
    for h in range(DIFF_HEADS):
        sl = slice(h * LANES, (h + 1) * LANES)
        kr = rope(k[:, sl])
        k_ref[0, :, sl] = kr
        k16_ref[0, h] = kr.astype(BF16)
    v = jnp.dot(xn, w_ref[:, 2 * DIFF_QK:], preferred_element_type=F32)
    v_ref[0] = v
    for h in range(DIFF_HEADS):
        v16_ref[0, h] = v[:, h * LANES:(h + 1) * LANES].astype(BF16)


def _qkv_call(x, g, w, tabs, tm):
    B, T, _ = x.shape
    row = pl.BlockSpec((1, tm, D_MODEL), lambda b, i: (b, i, 0))
    heads = pl.BlockSpec((1, DIFF_HEADS, tm, LANES), lambda b, i: (b, 0, i, 0))
    tab = pl.BlockSpec((tm, LANES), lambda b, i: (i, 0))
    head_major = jax.ShapeDtypeStruct((B, DIFF_HEADS, T, LANES), BF16)
    return pl.pallas_call(
        _qkv_body,
        grid=(B, T // tm),
        in_specs=[row, _const_spec((1, D_MODEL)), _const_spec(w.shape), tab, tab, tab],
        out_specs=[row, row, heads, heads, heads],
        out_shape=[jax.ShapeDtypeStruct((B, T, DIFF_QK), F32),
                   jax.ShapeDtypeStruct((B, T, DIFF_VAL), F32),
                   head_major, head_major, head_major],
        compiler_params=_params(2),
        name="qkv_proj",
    )(x, g, w, *tabs)


def _rope_tables(pos):
    half = ROT_DIM // 2
    lane = jnp.arange(LANES, dtype=jnp.int32) % DIFF_HD
    inv = ROPE_THETA ** (-(2 * (lane % half)).astype(F32) / ROT_DIM)
    inv = jnp.where(lane < ROT_DIM, inv, 0.0)
    ang = pos.astype(F32)[:, None] * inv[None, :]
    cos, sin = jnp.cos(ang), jnp.sin(ang)
    s1 = jnp.where(lane[None, :] < half, -sin, 0.0)
    s2 = jnp.where((lane[None, :] >= half) & (lane[None, :] < ROT_DIM), sin, 0.0)
    return cos, s1, s2


def _gla_body(x_ref, g_ref, w_ref, wa_ref, ba_ref, s0_ref, o_ref, s_ref, st_ref, *, chunk):
    t = pl.program_id(1)
    C = chunk

    @pl.when(t == 0)
    def _():
        for h in range(GLA_HEADS):
            st_ref[h] = s0_ref[0, h]

    ri = lax.broadcasted_iota(jnp.int32, (C, C), 0)
    ci = lax.broadcasted_iota(jnp.int32, (C, C), 1)
    tril = ri >= ci
    tril_b = jnp.where(tril, 1.0, 0.0).astype(BF16)

    heads = range(GLA_HEADS)
    ks = [slice(h * GLA_DK, (h + 1) * GLA_DK) for h in heads]
    vs = [slice(h * GLA_DV, (h + 1) * GLA_DV) for h in heads]

    n_chunks = x_ref.shape[1] // C
    xn = _rms_rows(x_ref[0], g_ref[...]).astype(BF16)
    gr = jnp.dot(xn, w_ref[:, 2 * GLA_KEY + GLA_VAL:], preferred_element_type=F32)
    z = jnp.dot(gr.astype(BF16), wa_ref[...], preferred_element_type=F32) + ba_ref[...]
    la_all = (jnp.minimum(z, 0.0) - jnp.log(1.0 + jnp.exp(-jnp.abs(z)))) * (1.0 / GLA_TAU)
    u_all = jnp.dot(xn, w_ref[:, 0:2 * GLA_KEY + GLA_VAL], preferred_element_type=F32)

    def operands(c):
        u, la = u_all[c * C:(c + 1) * C], la_all[c * C:(c + 1) * C]
        hi = la.astype(BF16)
        lo = (la - hi.astype(F32)).astype(BF16)
        b = (jnp.dot(tril_b, hi, preferred_element_type=F32)
             + jnp.dot(tril_b, lo, preferred_element_type=F32))
        r = b[C // 2 - 1:C // 2]
        bl = b[C - 1:C]
        q1 = u[:, 0:GLA_KEY] * GLA_SCALE * jnp.exp(b - r)
        qs = (q1 * jnp.exp(r)).astype(BF16)
        k1 = u[:, GLA_KEY:2 * GLA_KEY] * jnp.exp(r - b)
        kd = (k1 * jnp.exp(bl - r)).astype(BF16)
        v = u[:, 2 * GLA_KEY:2 * GLA_KEY + GLA_VAL].astype(BF16)
        return q1.astype(BF16), qs, k1.astype(BF16), kd, v, jnp.exp(bl)

    ops = [operands(c) for c in range(n_chunks)]
    a_raw = [[lax.dot_general(q1[:, ks[h]], k1[:, ks[h]], _NT, preferred_element_type=F32) for h in heads]
             for (q1, qs, k1, kd, v, dec) in ops]
    kv = [[lax.dot_general(kd[:, ks[h]], v[:, vs[h]], _TN, preferred_element_type=F32) for h in heads]
          for (q1, qs, k1, kd, v, dec) in ops]
    intra = [[jnp.dot(jnp.where(tril, a_raw[c][h], 0.0).astype(BF16), ops[c][4][:, vs[h]],
                      preferred_element_type=F32) for h in heads] for c in range(n_chunks)]

    def per_row(dec_h):
        col = jnp.broadcast_to(dec_h, (GLA_DK, GLA_DK)).T
        return jnp.concatenate([col] * (GLA_DV // GLA_DK), axis=1)

    for c in range(n_chunks):
        q1, qs, k1, kd, v, dec = ops[c]
        rows = slice(c * C, (c + 1) * C)
        for h in heads:
            st = st_ref[h]
            o_ref[0, rows, vs[h]] = intra[c][h] + jnp.dot(qs[:, ks[h]], st.astype(BF16),
                                                          preferred_element_type=F32)
            st_ref[h] = st * per_row(dec[:, ks[h]]) + kv[c][h]

    @pl.when(t == pl.num_programs(1) - 1)
    def _():
        for h in range(GLA_HEADS):
            s_ref[0, h] = st_ref[h]


def _gla_call(x, g, w, wa, ba, s0, layer, tt, chunk):
    B, T, _ = x.shape
    row = pl.BlockSpec((1, tt, D_MODEL), lambda b, i: (b, i, 0))
    state_shape = (1, GLA_HEADS, GLA_DK, GLA_DV)
    return pl.pallas_call(
        functools.partial(_gla_body, chunk=chunk),
        grid=(B, T // tt),
        in_specs=[row, _const_spec((1, D_MODEL)), _const_spec(w.shape), _const_spec(wa.shape),
                  _const_spec((1, GLA_KEY)), pl.BlockSpec(state_shape, lambda b, i: (layer * B + b, 0, 0, 0))],
        out_specs=[row, pl.BlockSpec(state_shape, lambda b, i: (b, 0, 0, 0))],
        out_shape=[jax.ShapeDtypeStruct((B, T, GLA_VAL), F32),
                   jax.ShapeDtypeStruct((B, GLA_HEADS, GLA_DK, GLA_DV), F32)],
        scratch_shapes=[pltpu.VMEM((GLA_HEADS, GLA_DK, GLA_DV), F32)],
        compiler_params=_params(2),
        name="gla",
    )(x, g, w, wa, ba, s0)


def _lam(lq1, lk1, lq2, lk2, lambda_init):
    return (jnp.exp(jnp.sum(lq1[...] * lk1[...], axis=-1, keepdims=True))
            - jnp.exp(jnp.sum(lq2[...] * lk2[...], axis=-1, keepdims=True)) + lambda_init)


def _split_maps(q):
    lane = lax.broadcasted_iota(jnp.int32, q.shape, 1)
    zero = jnp.zeros_like(q)
    return jnp.where(lane < DIFF_HD, q, zero), jnp.where(lane >= DIFF_HD, q, zero)


def _ones_column(rows):
    lane = lax.broadcasted_iota(jnp.int32, (rows, LANES), 1)
    return jnp.where(lane == 0, 1.0, 0.0).astype(BF16)


def _attn_body(lq1, lk1, lq2, lk2, q_ref, k_ref, v_ref, bias_ref, *rest, n, tq, lambda_init):
    o_ref, s_ref = rest[-2:]
    heads = q_ref.shape[1]
    nblk = tq // LANES
    hq = tq // 2

    def lane_fold(s):
        return functools.reduce(jnp.maximum, [s[:, c * LANES:(c + 1) * LANES] for c in range(s.shape[1] // LANES)])

    mx = {}
    for g in range(heads):
        qz = _split_maps(q_ref[0, g])
        for j in range(n):
            kt = k_ref[0, g, j * tq:(j + 1) * tq, :]
            for m in range(2):
                if j < n - 1:
                    s = lax.dot_general(qz[m], kt, _NT, preferred_element_type=F32)
                    s_ref[g, m, j] = s
                    fold = lane_fold(s)
                else:
                    top = (lax.dot_general(qz[m][:hq], kt[:hq], _NT, preferred_element_type=F32)
                           + bias_ref[:hq, :hq])
                    bot = lax.dot_general(qz[m][hq:], kt, _NT, preferred_element_type=F32) + bias_ref[hq:, :]
                    s_ref[g, m, j, :hq, :hq] = top
                    s_ref[g, m, j, hq:, :] = bot
                    fold = jnp.concatenate([lane_fold(top), lane_fold(bot)], axis=0)
                mx[g, m] = fold if j == 0 else jnp.maximum(mx[g, m], fold)
    ones = _ones_column(tq)
    lam = _lam(lq1, lk1, lq2, lk2, lambda_init)
    for g in range(heads):
        acc = []
        for m in range(2):
            mrow = jnp.broadcast_to(jnp.max(mx[g, m], axis=-1, keepdims=True), (tq, LANES))
            mrow = jnp.concatenate([mrow] * nblk, axis=1)
            a = None
            for j in range(n):
                vt = jnp.concatenate([v_ref[0, g, j * tq:(j + 1) * tq, :], ones], axis=1)
                if j < n - 1:
                    p = jnp.exp2(s_ref[g, m, j] - mrow).astype(BF16)
                    d = jnp.dot(p, vt, preferred_element_type=F32)
                else:
                    p_top = jnp.exp2(s_ref[g, m, j, :hq, :hq] - mrow[:hq, :hq]).astype(BF16)
                    p_bot = jnp.exp2(s_ref[g, m, j, hq:, :] - mrow[hq:]).astype(BF16)
                    d = jnp.concatenate([jnp.dot(p_top, vt[:hq], preferred_element_type=F32),
                                         jnp.dot(p_bot, vt, preferred_element_type=F32)], axis=0)
                a = d if a is None else a + d
            acc.append(a)
        o = (acc[0][:, 0:DIFF_VD] / acc[0][:, DIFF_VD:DIFF_VD + 1]
             - lam * (acc[1][:, 0:DIFF_VD] / acc[1][:, DIFF_VD:DIFF_VD + 1])).astype(o_ref.dtype)
        cols = slice(g * DIFF_VD, (g + 1) * DIFF_VD)
        if o_ref.shape[1] == tq:
            o_ref[0, :, cols] = o
        else:
            o_ref[0, (n - 1) * tq:n * tq, cols] = o
            o_ref[0, 0:(n - 1) * tq, cols] = jnp.zeros(((n - 1) * tq, DIFF_VD), o_ref.dtype)


ATTN_SCORE_BYTES = 32 * 1024 * 1024


def _attn_call(lams, q, k, v, bias, o_prev, n, tq, lambda_init):
    B, _, T, _ = q.shape
    assert tq % (2 * LANES) == 0 and (tq // 2) % CHUNK == 0, tq
    heads = DIFF_HEADS
    while heads > 1 and heads * 2 * n * tq * tq * 4 > ATTN_SCORE_BYTES:
        heads //= 2
    lam_spec = _const_spec((1, DIFF_HD))
    qs = pl.BlockSpec((1, heads, tq, LANES), lambda b, h: (b, h, n - 1, 0))
    kv = pl.BlockSpec((1, heads, n * tq, LANES), lambda b, h: (b, h, 0, 0))
    in_specs = [lam_spec] * 4 + [qs, kv, kv, _const_spec((tq, tq))]
    args = (*lams, q, k, v, bias)
    if o_prev is None:
        assert n * tq == T, (n, tq, T)
        aliases = {}
        out_spec = pl.BlockSpec((1, T, heads * DIFF_VD), lambda b, h: (b, 0, h))
    else:
        in_specs.append(pl.BlockSpec(memory_space=pl.ANY))
        aliases = {len(args): 0}
        args = args + (o_prev,)
        out_spec = pl.BlockSpec((1, tq, heads * DIFF_VD), lambda b, h: (b, n - 1, h))
    return pl.pallas_call(
        functools.partial(_attn_body, n=n, tq=tq, lambda_init=lambda_init),
        grid=(B, DIFF_HEADS // heads),
        in_specs=in_specs,
        out_specs=out_spec,
        out_shape=jax.ShapeDtypeStruct((B, T, DIFF_VAL), BF16),
        scratch_shapes=[pltpu.VMEM((heads, 2, n, tq, tq), F32)],
        input_output_aliases=aliases,
        compiler_params=_params(2),
        name=f"diff_attn_q{n - 1}",
    )(*args)


def _diag_bias(tq):
    r = jnp.arange(tq, dtype=jnp.int32) // CHUNK
    return jnp.where(r[:, None] >= r[None, :], 0.0, -jnp.inf).astype(F32)


def _attn_cache_body(lq1, lk1, lq2, lk2, q_ref, kc_ref, vc_ref, kn_ref, vn_ref, o_ref, *, lambda_init):
    T = q_ref.shape[2]
    P = kc_ref.shape[1] // DIFF_HEADS
    lam = _lam(lq1, lk1, lq2, lk2, lambda_init)
    for h in range(DIFF_HEADS):
        qz = jnp.concatenate(_split_maps(q_ref[0, h]), axis=0)
        kc = kc_ref[0, pl.ds(h, P, stride=DIFF_HEADS), :].astype(BF16)
        vc = vc_ref[0, pl.ds(h, P, stride=DIFF_HEADS), :].astype(BF16)
        sc = lax.dot_general(qz, kc, _NT, preferred_element_type=F32)
        sn = lax.dot_general(qz, kn_ref[0, h], _NT, preferred_element_type=F32)
        mx = jnp.maximum(jnp.max(sc, axis=-1, keepdims=True), jnp.max(sn, axis=-1, keepdims=True))
        pc = jnp.exp2(sc - mx)
        pn = jnp.exp2(sn - mx)
        l = jnp.sum(pc, axis=-1, keepdims=True) + jnp.sum(pn, axis=-1, keepdims=True)
        o = (jnp.dot(pc.astype(BF16), vc, preferred_element_type=F32)
             + jnp.dot(pn.astype(BF16), vn_ref[0, h], preferred_element_type=F32)) / l
        o_ref[0, :, h * DIFF_VD:(h + 1) * DIFF_VD] = o[0:T] - lam * o[T:2 * T]


def _attn_cache_call(lams, q, kc, vc, kn, vn, layer, lambda_init):
    B, _, T, _ = q.shape
    lam_spec = _const_spec((1, DIFF_HD))
    new = pl.BlockSpec((1, DIFF_HEADS, T, LANES), lambda b: (b, 0, 0, 0))
    past = pl.BlockSpec((1, kc.shape[1], LANES), lambda b: (layer * B + b, 0, 0))
    return pl.pallas_call(
        functools.partial(_attn_cache_body, lambda_init=lambda_init),
        grid=(B,),
        in_specs=[lam_spec] * 4 + [new, past, past, new, new],
        out_specs=pl.BlockSpec((1, T, DIFF_VAL), lambda b: (b, 0, 0)),
        out_shape=jax.ShapeDtypeStruct((B, T, DIFF_VAL), F32),
        compiler_params=_params(1),
        name="diff_attn_cache",
    )(*lams, q, kc, vc, kn, vn)


MERGE_ROWS = 256


def _merge_body(x_ref, og_ref, od_ref, gpre_ref, wg_ref, ggla_ref, gsub_ref, wpg_ref, wpd_ref, wo_ref,
                gpost_ref, y_ref, yg_s, yd_s, *, lambda_init):
    tm = x_ref.shape[0]
    blocks = [slice(r, r + min(MERGE_ROWS, tm)) for r in range(0, tm, min(MERGE_ROWS, tm))]
    gates = [jnp.dot(_rms_rows(x_ref[rows], gpre_ref[...]).astype(BF16), wg_ref[...],
                     preferred_element_type=F32) for rows in blocks]
    for rows, gt in zip(blocks, gates):
        og, od = og_ref[rows], od_ref[rows].astype(F32)
        for h in range(GLA_HEADS):
            sl = slice(h * GLA_DV, (h + 1) * GLA_DV)
            yg_s[rows, sl] = (_rms_rows(og[:, sl], ggla_ref[...]) * jax.nn.silu(gt[:, sl])).astype(BF16)
        for h in range(DIFF_HEADS):
            sl = slice(h * DIFF_VD, (h + 1) * DIFF_VD)
            dg = gt[:, GLA_VAL + h * DIFF_VD:GLA_VAL + (h + 1) * DIFF_VD]
            yd_s[rows, sl] = (_rms_rows(od[:, sl], gsub_ref[...]) * (1.0 - lambda_init)
                              * jax.nn.silu(dg)).astype(BF16)
    merged = []
    for rows, gt in zip(blocks, gates):
        ma = gt[:, GLA_VAL + DIFF_VAL:GLA_VAL + DIFF_VAL + D_MODEL]
        mb = gt[:, GLA_VAL + DIFF_VAL + D_MODEL:]
        merged.append(jax.nn.sigmoid(ma) * jnp.dot(yg_s[rows], wpg_ref[...], preferred_element_type=F32)
                      + jax.nn.sigmoid(mb) * jnp.dot(yd_s[rows], wpd_ref[...], preferred_element_type=F32))
    for rows, mg in zip(blocks, merged):
        z = jnp.dot(mg.astype(BF16), wo_ref[...], preferred_element_type=F32)
        y_ref[rows] = x_ref[rows] + _rms_rows(z, gpost_ref[...])


def _merge_call(x, og, od, gpre, wg, ggla, gsub, wpg, wpd, wo, gpost, tm, lambda_init):
    N = x.shape[0]
    row = pl.BlockSpec((tm, D_MODEL), lambda i: (i, 0))
    return pl.pallas_call(
        functools.partial(_merge_body, lambda_init=lambda_init),
        grid=(N // tm,),
        in_specs=[row, row, row, _const_spec((1, D_MODEL)), _const_spec(wg.shape), _const_spec((1, GLA_DV)),
                  _const_spec((1, DIFF_VD)), _const_spec(wpg.shape), _const_spec(wpd.shape),
                  _const_spec(wo.shape), _const_spec((1, D_MODEL))],
        out_specs=row,
        out_shape=jax.ShapeDtypeStruct((N, D_MODEL), F32),
        scratch_shapes=[pltpu.VMEM((tm, GLA_VAL), BF16), pltpu.VMEM((tm, DIFF_VAL), BF16)],
        compiler_params=_params(1),
        name="merge",
    )(x, og, od, gpre, wg, ggla, gsub, wpg, wpd, wo, gpost)


def _tile(n, target):
    t = min(n, target)
    while n % t:
        t -= 1
    return t


def kernel(x_prompt, x_sample, cache_diff_k, cache_diff_v, state_gla, pre_norm_g, w_in, gla_w_a2, gla_b_a,
           gla_norm_g, diff_lambda_q1, diff_lambda_k1, diff_lambda_q2, diff_lambda_k2, diff_subln_g,
           w_proj_gla, w_proj_diff, w_out, post_norm_g):
    depth = w_in.shape[0]
    past_len = cache_diff_k.shape[2]
    hp, hs = x_prompt, x_sample
    B, T, _ = hp.shape
    Bs, Ts, _ = hs.shape
    cache_k = cache_diff_k.reshape(depth * Bs, past_len * DIFF_HEADS, 2 * DIFF_HD)
    cache_v = cache_diff_v.reshape(depth * Bs, past_len * DIFF_HEADS, DIFF_VD)
    state_in = state_gla.astype(F32).reshape(depth * Bs, GLA_HEADS, GLA_DK, GLA_DV)
    zero_state = jnp.zeros((B, GLA_HEADS, GLA_DK, GLA_DV), F32)
    outs = [[] for _ in range(6)]
    offs = [0]
    for s in IN_SIZES:
        offs.append(offs[-1] + s)
    for l in range(depth):
        lambda_init = 0.8 - 0.6 * math.exp(-0.3 * l)
        def cols(first, last):
            return w_in[l, :, offs[first]:offs[last + 1]].astype(BF16)

        w_gla = jnp.concatenate([cols(0, 2), jnp.pad(cols(3, 3), ((0, 0), (0, LANES - GLA_RANK)))], axis=1)
        w_qkv = cols(5, 7)
        w_gate = jnp.concatenate([cols(4, 4), cols(8, 10)], axis=1)
        wa = jnp.pad(gla_w_a2[l], ((0, LANES - GLA_RANK), (0, 0))).astype(BF16)
        ba = gla_b_a[l][None]
        gpre = pre_norm_g[l][None]
        lams = (diff_lambda_q1[l][None], diff_lambda_k1[l][None], diff_lambda_q2[l][None], diff_lambda_k2[l][None])
        merge_w = (gpre, w_gate, gla_norm_g[l][None], diff_subln_g[l][None], w_proj_gla[l].astype(BF16),
                   w_proj_diff[l].astype(BF16), w_out[l].astype(BF16), post_norm_g[l][None])

        k, v, q16, k16, v16 = _qkv_call(hp, gpre, w_qkv, _rope_tables(jnp.arange(T, dtype=jnp.int32)),
                                        _tile(T, 512))
        o_gla, s_p = _gla_call(hp, gpre, w_gla, wa, ba, zero_state, 0, _tile(T, 1024), _tile(T, 128))
        tq = _tile(T, 512)
        bias = _diag_bias(tq)
        o_diff = None
        for n in range(T // tq, 0, -1):
            o_diff = _attn_call(lams, q16, k16, v16, bias, o_diff, n, tq, lambda_init)
        hp = _merge_call(hp.reshape(B * T, D_MODEL), o_gla.reshape(B * T, GLA_VAL),
                         o_diff.reshape(B * T, DIFF_VAL), *merge_w, _tile(B * T, 512),
                         lambda_init).reshape(B, T, D_MODEL)
        outs[0].append(k.reshape(B, T, DIFF_HEADS, DIFF_VD))
        outs[1].append(v.reshape(B, T, DIFF_HEADS, DIFF_VD))
        outs[2].append(s_p)

        pos_s = past_len + jnp.arange(Ts, dtype=jnp.int32)
        k, v, q16, k16, v16 = _qkv_call(hs, gpre, w_qkv, _rope_tables(pos_s), Ts)
        o_gla, s_s = _gla_call(hs, gpre, w_gla, wa, ba, state_in, l, Ts, Ts)
        o_diff = _attn_cache_call(lams, q16, cache_k, cache_v, k16, v16, l, lambda_init)
        hs = _merge_call(hs.reshape(Bs * Ts, D_MODEL), o_gla.reshape(Bs * Ts, GLA_VAL),
                         o_diff.reshape(Bs * Ts, DIFF_VAL), *merge_w, _tile(Bs * Ts, 256),
                         lambda_init).reshape(Bs, Ts, D_MODEL)
        outs[3].append(k.reshape(Bs, Ts, DIFF_HEADS, DIFF_VD))
        outs[4].append(v.reshape(Bs, Ts, DIFF_HEADS, DIFF_VD))
        outs[5].append(s_s)

    return (hp, hs, jnp.stack(outs[0]), jnp.stack(outs[1]), jnp.stack(outs[2]),
            jnp.stack(outs[3]), jnp.stack(outs[4]), jnp.stack(outs[5]))
```

```python
import functools
import math

import jax
import jax.numpy as jnp
from jax import lax
from jax.experimental import pallas as pl
from jax.experimental.pallas import tpu as pltpu

F32 = jnp.float32
BF16 = jnp.bfloat16

D_MODEL = 1024
EPS = 1e-6
CHUNK = 64

GLA_HEADS = 4
GLA_DK = 128
GLA_DV = 256
GLA_KEY = GLA_HEADS * GLA_DK
GLA_VAL = GLA_HEADS * GLA_DV
GLA_RANK = 16
GLA_TAU = 16.0
GLA_SCALE = GLA_DK ** -0.5

DIFF_HEADS = 8
DIFF_HD = 64
DIFF_VD = 128
DIFF_QK = DIFF_HEADS * 2 * DIFF_HD
DIFF_VAL = DIFF_HEADS * DIFF_VD
ROT_DIM = DIFF_HD // 4
ROPE_THETA = 500000.0
QK_SCALE = DIFF_HD ** -0.5
Q_PRESCALE = QK_SCALE * math.log2(math.e)

IN_SIZES = (GLA_KEY, GLA_KEY, GLA_VAL, GLA_RANK, GLA_VAL,
            DIFF_QK, DIFF_QK, DIFF_VAL, DIFF_VAL, D_MODEL, D_MODEL)

LANES = 128
VMEM_LIMIT = 56 * 1024 * 1024

_NT = (((1,), (1,)), ((), ()))
_TN = (((0,), (0,)), ((), ()))


def _rms_rows(x, g):
    return x * lax.rsqrt(jnp.mean(x * x, axis=-1, keepdims=True) + EPS) * g


def _params(n_axes):
    return pltpu.CompilerParams(dimension_semantics=("arbitrary",) * n_axes,
                                vmem_limit_bytes=VMEM_LIMIT)


def _const_spec(shape):
    zeros = (0,) * len(shape)
    return pl.BlockSpec(shape, lambda *_: zeros, pipeline_mode=pl.Buffered(1))


def _qkv_body(x_ref, g_ref, w_ref, c_ref, s1_ref, s2_ref, k_ref, v_ref, q16_ref, k16_ref, v16_ref):
    xn = _rms_rows(x_ref[0], g_ref[...]).astype(BF16)
    c, s1, s2 = c_ref[...], s1_ref[...], s2_ref[...]

    def rope(t):
        return t * c + pltpu.roll(t, LANES - ROT_DIM // 2, 1) * s1 + pltpu.roll(t, ROT_DIM // 2, 1) * s2

    q = jnp.dot(xn, w_ref[:, 0:DIFF_QK], preferred_element_type=F32)
    for h in range(DIFF_HEADS):
        sl = slice(h * LANES, (h + 1) * LANES)
        q16_ref[0, h] = (rope(q[:, sl]) * Q_PRESCALE).astype(BF16)
    k = jnp.dot(xn, w_ref[:, DIFF_QK:2 * DIFF_QK], preferred_element_type=F32)
    for h in range(DIFF_HEADS):
        sl = slice(h * LANES, (h + 1) * LANES)
        kr = rope(k[:, sl])
        k_ref[0, :, sl] = kr
        k16_ref[0, h] = kr.astype(BF16)
    v = jnp.dot(xn, w_ref[:, 2 * DIFF_QK:], preferred_element_type=F32)
    v_ref[0] = v
    for h in range(DIFF_HEADS):
        v16_ref[0, h] = v[:, h * LANES:(h + 1) * LANES].astype(BF16)


def _qkv_call(x, g, w, tabs, tm):
    B, T, _ = x.shape
    row = pl.BlockSpec((1, tm, D_MODEL), lambda b, i: (b, i, 0))
    heads = pl.BlockSpec((1, DIFF_HEADS, tm, LANES), lambda b, i: (b, 0, i, 0))
    tab = pl.BlockSpec((tm, LANES), lambda b, i: (i, 0))
    head_major = jax.ShapeDtypeStruct((B, DIFF_HEADS, T, LANES), BF16)
    return pl.pallas_call(
        _qkv_body,
        grid=(B, T // tm),
        in_specs=[row, _const_spec((1, D_MODEL)), _const_spec(w.shape), tab, tab, tab],
        out_specs=[row, row, heads, heads, heads],
        out_shape=[jax.ShapeDtypeStruct((B, T, DIFF_QK), F32),
                   jax.ShapeDtypeStruct((B, T, DIFF_VAL), F32),
                   head_major, head_major, head_major],
        compiler_params=_params(2),
        name="qkv_proj",
    )(x, g, w, *tabs)


def _rope_tables(pos):
    half = ROT_DIM // 2
    lane = jnp.arange(LANES, dtype=jnp.int32) % DIFF_HD
    inv = ROPE_THETA ** (-(2 * (lane % half)).astype(F32) / ROT_DIM)
    inv = jnp.where(lane < ROT_DIM, inv, 0.0)
    ang = pos.astype(F32)[:, None] * inv[None, :]
    cos, sin = jnp.cos(ang), jnp.sin(ang)
    s1 = jnp.where(lane[None, :] < half, -sin, 0.0)
    s2 = jnp.where((lane[None, :] >= half) & (lane[None, :] < ROT_DIM), sin, 0.0)
    return cos, s1, s2


def _gla_body(x_ref, g_ref, w_ref, wa_ref, ba_ref, s0_ref, o_ref, s_ref, st_ref, *, chunk):
    t = pl.program_id(1)
    C = chunk

    @pl.when(t == 0)
    def _():
        for h in range(GLA_HEADS):
            st_ref[h] = s0_ref[0, h].T

    ri = lax.broadcasted_iota(jnp.int32, (C, C), 0)
    ci = lax.broadcasted_iota(jnp.int32, (C, C), 1)
    tril = ri >= ci
    tril_b = jnp.where(tril, 1.0, 0.0).astype(BF16)

    heads = range(GLA_HEADS)
    ks = [slice(h * GLA_DK, (h + 1) * GLA_DK) for h in heads]
    vs = [slice(h * GLA_DV, (h + 1) * GLA_DV) for h in heads]

    n_chunks = x_ref.shape[1] // C
    xn = _rms_rows(x_ref[0], g_ref[...]).astype(BF16)
    gr = jnp.dot(xn, w_ref[:, 2 * GLA_KEY + GLA_VAL:], preferred_element_type=F32)
    z = jnp.dot(gr.astype(BF16), wa_ref[...], preferred_element_type=F32) + ba_ref[...]
    la_all = (jnp.minimum(z, 0.0) - jnp.log(1.0 + jnp.exp(-jnp.abs(z)))) * (1.0 / GLA_TAU)
    u_all = jnp.dot(xn, w_ref[:, 0:2 * GLA_KEY + GLA_VAL], preferred_element_type=F32)

    def operands(c):
        u, la = u_all[c * C:(c + 1) * C], la_all[c * C:(c + 1) * C]
        hi = la.astype(BF16)
        lo = (la - hi.astype(F32)).astype(BF16)
        b = (jnp.dot(tril_b, hi, preferred_element_type=F32)
             + jnp.dot(tril_b, lo, preferred_element_type=F32))
        r = b[C // 2 - 1:C // 2]
        bl = b[C - 1:C]
        q1 = u[:, 0:GLA_KEY] * GLA_SCALE * jnp.exp(b - r)
        qs = (q1 * jnp.exp(r)).astype(BF16)
        k1 = u[:, GLA_KEY:2 * GLA_KEY] * jnp.exp(r - b)
        kd = (k1 * jnp.exp(bl - r)).astype(BF16)
        v = u[:, 2 * GLA_KEY:2 * GLA_KEY + GLA_VAL].astype(BF16)
        return q1.astype(BF16), qs, k1.astype(BF16), kd, v, jnp.exp(bl)

    ops = [operands(c) for c in range(n_chunks)]
    a_raw = [[lax.dot_general(q1[:, ks[h]], k1[:, ks[h]], _NT, preferred_element_type=F32) for h in heads]
             for (q1, qs, k1, kd, v, dec) in ops]
    kv = [[lax.dot_general(v[:, vs[h]], kd[:, ks[h]], _TN, preferred_element_type=F32) for h in heads]
          for (q1, qs, k1, kd, v, dec) in ops]
    intra = [[jnp.dot(jnp.where(tril, a_raw[c][h], 0.0).astype(BF16), ops[c][4][:, vs[h]],
                      preferred_element_type=F32) for h in heads] for c in range(n_chunks)]
    for c in range(n_chunks):
        q1, qs, k1, kd, v, dec = ops[c]
        rows = slice(c * C, (c + 1) * C)
        for h in heads:
            st = st_ref[h]
            o_ref[0, rows, vs[h]] = intra[c][h] + lax.dot_general(qs[:, ks[h]], st.astype(BF16), _NT,
                                                                  preferred_element_type=F32)
            st_ref[h] = st * dec[:, ks[h]] + kv[c][h]

    @pl.when(t == pl.num_programs(1) - 1)
    def _():
        for h in range(GLA_HEADS):
            s_ref[0, h] = st_ref[h].T


def _gla_call(x, g, w, wa, ba, s0, layer, tt, chunk):
    B, T, _ = x.shape
    row = pl.BlockSpec((1, tt, D_MODEL), lambda b, i: (b, i, 0))
    state_shape = (1, GLA_HEADS, GLA_DK, GLA_DV)
    return pl.pallas_call(
        functools.partial(_gla_body, chunk=chunk),
        grid=(B, T // tt),
        in_specs=[row, _const_spec((1, D_MODEL)), _const_spec(w.shape), _const_spec(wa.shape),
                  _const_spec((1, GLA_KEY)), pl.BlockSpec(state_shape, lambda b, i: (layer * B + b, 0, 0, 0))],
        out_specs=[row, pl.BlockSpec(state_shape, lambda b, i: (b, 0, 0, 0))],
        out_shape=[jax.ShapeDtypeStruct((B, T, GLA_VAL), F32),
                   jax.ShapeDtypeStruct((B, GLA_HEADS, GLA_DK, GLA_DV), F32)],
        scratch_shapes=[pltpu.VMEM((GLA_HEADS, GLA_DV, GLA_DK), F32)],
        compiler_params=_params(2),
        name="gla",
    )(x, g, w, wa, ba, s0)


def _lam(lq1, lk1, lq2, lk2, lambda_init):
    return (jnp.exp(jnp.sum(lq1[...] * lk1[...], axis=-1, keepdims=True))
            - jnp.exp(jnp.sum(lq2[...] * lk2[...], axis=-1, keepdims=True)) + lambda_init)


def _split_maps(q):
    lane = lax.broadcasted_iota(jnp.int32, q.shape, 1)
    zero = jnp.zeros_like(q)
    return jnp.where(lane < DIFF_HD, q, zero), jnp.where(lane >= DIFF_HD, q, zero)


def _attn_body(lq1, lk1, lq2, lk2, q_ref, k_ref, v_ref, bias_ref, *rest, n, tq, lambda_init):
    o_ref, s_ref = rest[-2:]
    heads = q_ref.shape[1]
    nblk = tq // LANES
    hq = tq // 2

    def lane_fold(s):
        return functools.reduce(jnp.maximum, [s[:, c * LANES:(c + 1) * LANES] for c in range(s.shape[1] // LANES)])

    mx = {}
    for g in range(heads):
        qz = _split_maps(q_ref[0, g])
        for j in range(n):
            kt = k_ref[0, g, j * tq:(j + 1) * tq, :]
            for m in range(2):
                if j < n - 1:
                    s = lax.dot_general(qz[m], kt, _NT, preferred_element_type=F32)
                    s_ref[g, m, j] = s
                    fold = lane_fold(s)
                else:
                    top = (lax.dot_general(qz[m][:hq], kt[:hq], _NT, preferred_element_type=F32)
                           + bias_ref[:hq, :hq])
                    bot = lax.dot_general(qz[m][hq:], kt, _NT, preferred_element_type=F32) + bias_ref[hq:, :]
                    s_ref[g, m, j, :hq, :hq] = top
                    s_ref[g, m, j, hq:, :] = bot
                    fold = jnp.concatenate([lane_fold(top), lane_fold(bot)], axis=0)
                mx[g, m] = fold if j == 0 else jnp.maximum(mx[g, m], fold)
    ones = jnp.ones((tq, LANES), BF16)
    lam = _lam(lq1, lk1, lq2, lk2, lambda_init)
    for g in range(heads):
        acc = []
        for m in range(2):
            mrow = jnp.broadcast_to(jnp.max(mx[g, m], axis=-1, keepdims=True), (tq, LANES))
            mrow = jnp.concatenate([mrow] * nblk, axis=1)
            a = None
            for j in range(n):
                vt = jnp.concatenate([v_ref[0, g, j * tq:(j + 1) * tq, :], ones], axis=1)
                if j < n - 1:
                    p = jnp.exp2(s_ref[g, m, j] - mrow).astype(BF16)
                    d = jnp.dot(p, vt, preferred_element_type=F32)
                else:
                    p_top = jnp.exp2(s_ref[g, m, j, :hq, :hq] - mrow[:hq, :hq]).astype(BF16)
                    p_bot = jnp.exp2(s_ref[g, m, j, hq:, :] - mrow[hq:]).astype(BF16)
                    d = jnp.concatenate([jnp.dot(p_top, vt[:hq], preferred_element_type=F32),
                                         jnp.dot(p_bot, vt, preferred_element_type=F32)], axis=0)
                a = d if a is None else a + d
            acc.append(a)
        o = (acc[0][:, 0:DIFF_VD] / acc[0][:, DIFF_VD:]
             - lam * (acc[1][:, 0:DIFF_VD] / acc[1][:, DIFF_VD:])).astype(o_ref.dtype)
        cols = slice(g * DIFF_VD, (g + 1) * DIFF_VD)
        if o_ref.shape[1] == tq:
            o_ref[0, :, cols] = o
        else:
            o_ref[0, (n - 1) * tq:n * tq, cols] = o
            o_ref[0, 0:(n - 1) * tq, cols] = jnp.zeros(((n - 1) * tq, DIFF_VD), o_ref.dtype)


ATTN_SCORE_BYTES = 32 * 1024 * 1024


def _attn_call(lams, q, k, v, bias, o_prev, n, tq, lambda_init):
    B, _, T, _ = q.shape
    assert tq % (2 * LANES) == 0 and (tq // 2) % CHUNK == 0, tq
    heads = DIFF_HEADS
    while heads > 1 and heads * 2 * n * tq * tq * 4 > ATTN_SCORE_BYTES:
        heads //= 2
    lam_spec = _const_spec((1, DIFF_HD))
    qs = pl.BlockSpec((1, heads, tq, LANES), lambda b, h: (b, h, n - 1, 0))
    kv = pl.BlockSpec((1, heads, n * tq, LANES), lambda b, h: (b, h, 0, 0))
    in_specs = [lam_spec] * 4 + [qs, kv, kv, _const_spec((tq, tq))]
    args = (*lams, q, k, v, bias)
    if o_prev is None:
        assert n * tq == T, (n, tq, T)
        aliases = {}
        out_spec = pl.BlockSpec((1, T, heads * DIFF_VD), lambda b, h: (b, 0, h))
    else:
        in_specs.append(pl.BlockSpec(memory_space=pl.ANY))
        aliases = {len(args): 0}
        args = args + (o_prev,)
        out_spec = pl.BlockSpec((1, tq, heads * DIFF_VD), lambda b, h: (b, n - 1, h))
    return pl.pallas_call(
        functools.partial(_attn_body, n=n, tq=tq, lambda_init=lambda_init),
        grid=(B, DIFF_HEADS // heads),
        in_specs=in_specs,
        out_specs=out_spec,
        out_shape=jax.ShapeDtypeStruct((B, T, DIFF_VAL), BF16),
        scratch_shapes=[pltpu.VMEM((heads, 2, n, tq, tq), F32)],
        input_output_aliases=aliases,
        compiler_params=_params(2),
        name=f"diff_attn_q{n - 1}",
    )(*args)


def _diag_bias(tq):
    r = jnp.arange(tq, dtype=jnp.int32) // CHUNK
    return jnp.where(r[:, None] >= r[None, :], 0.0, -jnp.inf).astype(F32)


def _attn_cache_body(lq1, lk1, lq2, lk2, q_ref, kc_ref, vc_ref, kn_ref, vn_ref, o_ref, *, lambda_init):
    T = q_ref.shape[2]
    P = kc_ref.shape[1] // DIFF_HEADS
    lam = _lam(lq1, lk1, lq2, lk2, lambda_init)
    for h in range(DIFF_HEADS):
        qz = jnp.concatenate(_split_maps(q_ref[0, h]), axis=0)
        kc = kc_ref[0, pl.ds(h, P, stride=DIFF_HEADS), :].astype(BF16)
        vc = vc_ref[0, pl.ds(h, P, stride=DIFF_HEADS), :].astype(BF16)
        sc = lax.dot_general(qz, kc, _NT, preferred_element_type=F32)
        sn = lax.dot_general(qz, kn_ref[0, h], _NT, preferred_element_type=F32)
        mx = jnp.maximum(jnp.max(sc, axis=-1, keepdims=True), jnp.max(sn, axis=-1, keepdims=True))
        pc = jnp.exp2(sc - mx)
        pn = jnp.exp2(sn - mx)
        l = jnp.sum(pc, axis=-1, keepdims=True) + jnp.sum(pn, axis=-1, keepdims=True)
        o = (jnp.dot(pc.astype(BF16), vc, preferred_element_type=F32)
             + jnp.dot(pn.astype(BF16), vn_ref[0, h], preferred_element_type=F32)) / l
        o_ref[0, :, h * DIFF_VD:(h + 1) * DIFF_VD] = o[0:T] - lam * o[T:2 * T]


def _attn_cache_call(lams, q, kc, vc, kn, vn, layer, lambda_init):
    B, _, T, _ = q.shape
    lam_spec = _const_spec((1, DIFF_HD))
    new = pl.BlockSpec((1, DIFF_HEADS, T, LANES), lambda b: (b, 0, 0, 0))
    past = pl.BlockSpec((1, kc.shape[1], LANES), lambda b: (layer * B + b, 0, 0))
    return pl.pallas_call(
        functools.partial(_attn_cache_body, lambda_init=lambda_init),
        grid=(B,),
        in_specs=[lam_spec] * 4 + [new, past, past, new, new],
        out_specs=pl.BlockSpec((1, T, DIFF_VAL), lambda b: (b, 0, 0)),
        out_shape=jax.ShapeDtypeStruct((B, T, DIFF_VAL), F32),
        compiler_params=_params(1),
        name="diff_attn_cache",
    )(*lams, q, kc, vc, kn, vn)


MERGE_ROWS = 256


def _merge_body(x_ref, og_ref, od_ref, gpre_ref, wg_ref, ggla_ref, gsub_ref, wpg_ref, wpd_ref, wo_ref,
                gpost_ref, y_ref, yg_s, yd_s, *, lambda_init):
    tm = x_ref.shape[0]
    blocks = [slice(r, r + min(MERGE_ROWS, tm)) for r in range(0, tm, min(MERGE_ROWS, tm))]
    gates = [jnp.dot(_rms_rows(x_ref[rows], gpre_ref[...]).astype(BF16), wg_ref[...],
                     preferred_element_type=F32) for rows in blocks]
    for rows, gt in zip(blocks, gates):
        og, od = og_ref[rows], od_ref[rows].astype(F32)
        for h in range(GLA_HEADS):
            sl = slice(h * GLA_DV, (h + 1) * GLA_DV)
            yg_s[rows, sl] = (_rms_rows(og[:, sl], ggla_ref[...]) * jax.nn.silu(gt[:, sl])).astype(BF16)
        for h in range(DIFF_HEADS):
            sl = slice(h * DIFF_VD, (h + 1) * DIFF_VD)
            dg = gt[:, GLA_VAL + h * DIFF_VD:GLA_VAL + (h + 1) * DIFF_VD]
            yd_s[rows, sl] = (_rms_rows(od[:, sl], gsub_ref[...]) * (1.0 - lambda_init)
                              * jax.nn.silu(dg)).astype(BF16)
    merged = []
    for rows, gt in zip(blocks, gates):
        ma = gt[:, GLA_VAL + DIFF_VAL:GLA_VAL + DIFF_VAL + D_MODEL]
        mb = gt[:, GLA_VAL + DIFF_VAL + D_MODEL:]
        merged.append(jax.nn.sigmoid(ma) * jnp.dot(yg_s[rows], wpg_ref[...], preferred_element_type=F32)
                      + jax.nn.sigmoid(mb) * jnp.dot(yd_s[rows], wpd_ref[...], preferred_element_type=F32))
    for rows, mg in zip(blocks, merged):
        z = jnp.dot(mg.astype(BF16), wo_ref[...], preferred_element_type=F32)
        y_ref[rows] = x_ref[rows] + _rms_rows(z, gpost_ref[...])


def _merge_call(x, og, od, gpre, wg, ggla, gsub, wpg, wpd, wo, gpost, tm, lambda_init):
    N = x.shape[0]
    row = pl.BlockSpec((tm, D_MODEL), lambda i: (i, 0))
    return pl.pallas_call(
        functools.partial(_merge_body, lambda_init=lambda_init),
        grid=(N // tm,),
        in_specs=[row, row, row, _const_spec((1, D_MODEL)), _const_spec(wg.shape), _const_spec((1, GLA_DV)),
                  _const_spec((1, DIFF_VD)), _const_spec(wpg.shape), _const_spec(wpd.shape),
                  _const_spec(wo.shape), _const_spec((1, D_MODEL))],
        out_specs=row,
        out_shape=jax.ShapeDtypeStruct((N, D_MODEL), F32),
        scratch_shapes=[pltpu.VMEM((tm, GLA_VAL), BF16), pltpu.VMEM((tm, DIFF_VAL), BF16)],
        compiler_params=_params(1),
        name="merge",
    )(x, og, od, gpre, wg, ggla, gsub, wpg, wpd, wo, gpost)


def _tile(n, target):
    t = min(n, target)
    while n % t:
        t -= 1
    return t


def kernel(x_prompt, x_sample, cache_diff_k, cache_diff_v, state_gla, pre_norm_g, w_in, gla_w_a2, gla_b_a,
           gla_norm_g, diff_lambda_q1, diff_lambda_k1, diff_lambda_q2, diff_lambda_k2, diff_subln_g,
           w_proj_gla, w_proj_diff, w_out, post_norm_g):
    depth = w_in.shape[0]
    past_len = cache_diff_k.shape[2]
    hp, hs = x_prompt, x_sample
    B, T, _ = hp.shape
    Bs, Ts, _ = hs.shape
    cache_k = cache_diff_k.reshape(depth * Bs, past_len * DIFF_HEADS, 2 * DIFF_HD)
    cache_v = cache_diff_v.reshape(depth * Bs, past_len * DIFF_HEADS, DIFF_VD)
    state_in = state_gla.astype(F32).reshape(depth * Bs, GLA_HEADS, GLA_DK, GLA_DV)
    zero_state = jnp.zeros((B, GLA_HEADS, GLA_DK, GLA_DV), F32)
    outs = [[] for _ in range(6)]
    offs = [0]
    for s in IN_SIZES:
        offs.append(offs[-1] + s)
    for l in range(depth):
        lambda_init = 0.8 - 0.6 * math.exp(-0.3 * l)
        def cols(first, last):
            return w_in[l, :, offs[first]:offs[last + 1]].astype(BF16)

        w_gla = jnp.concatenate([cols(0, 2), jnp.pad(cols(3, 3), ((0, 0), (0, LANES - GLA_RANK)))], axis=1)
        w_qkv = cols(5, 7)
        w_gate = jnp.concatenate([cols(4, 4), cols(8, 10)], axis=1)
        wa = jnp.pad(gla_w_a2[l], ((0, LANES - GLA_RANK), (0, 0))).astype(BF16)
        ba = gla_b_a[l][None]
        gpre = pre_norm_g[l][None]
        lams = (diff_lambda_q1[l][None], diff_lambda_k1[l][None], diff_lambda_q2[l][None], diff_lambda_k2[l][None])
        merge_w = (gpre, w_gate, gla_norm_g[l][None], diff_subln_g[l][None], w_proj_gla[l].astype(BF16),
                   w_proj_diff[l].astype(BF16), w_out[l].astype(BF16), post_norm_g[l][None])

        k, v, q16, k16, v16 = _qkv_call(hp, gpre, w_qkv, _rope_tables(jnp.arange(T, dtype=jnp.int32)),
                                        _tile(T, 512))
        o_gla, s_p = _gla_call(hp, gpre, w_gla, wa, ba, zero_state, 0, _tile(T, 1024), _tile(T, 128))
        tq = _tile(T, 512)
        bias = _diag_bias(tq)
        o_diff = None
        for n in range(T // tq, 0, -1):
            o_diff = _attn_call(lams, q16, k16, v16, bias, o_diff, n, tq, lambda_init)
        hp = _merge_call(hp.reshape(B * T, D_MODEL), o_gla.reshape(B * T, GLA_VAL),
                         o_diff.reshape(B * T, DIFF_VAL), *merge_w, _tile(B * T, 512),
                         lambda_init).reshape(B, T, D_MODEL)
        outs[0].append(k.reshape(B, T, DIFF_HEADS, DIFF_VD))
        outs[1].append(v.reshape(B, T, DIFF_HEADS, DIFF_VD))
        outs[2].append(s_p)

        pos_s = past_len + jnp.arange(Ts, dtype=jnp.int32)
        k, v, q16, k16, v16 = _qkv_call(hs, gpre, w_qkv, _rope_tables(pos_s), Ts)
        o_gla, s_s = _gla_call(hs, gpre, w_gla, wa, ba, state_in, l, Ts, Ts)
        o_diff = _attn_cache_call(lams, q16, cache_k, cache_v, k16, v16, l, lambda_init)
        hs = _merge_call(hs.reshape(Bs * Ts, D_MODEL), o_gla.reshape(Bs * Ts, GLA_VAL),
                         o_diff.reshape(Bs * Ts, DIFF_VAL), *merge_w, _tile(Bs * Ts, 256),
                         lambda_init).reshape(Bs, Ts, D_MODEL)
        outs[3].append(k.reshape(Bs, Ts, DIFF_HEADS, DIFF_VD))
        outs[4].append(v.reshape(Bs, Ts, DIFF_HEADS, DIFF_VD))
        outs[5].append(s_s)

    return (hp, hs, jnp.stack(outs[0]), jnp.stack(outs[1]), jnp.stack(outs[2]),
            jnp.stack(outs[3]), jnp.stack(outs[4]), jnp.stack(outs[5]))
```

```python
import functools
import math

import jax
import jax.numpy as jnp
from jax import lax
from jax.experimental import pallas as pl
from jax.experimental.pallas import tpu as pltpu

F32 = jnp.float32
BF16 = jnp.bfloat16

D_MODEL = 1024
EPS = 1e-6
CHUNK = 64

GLA_HEADS = 4
GLA_DK = 128
GLA_DV = 256
GLA_KEY = GLA_HEADS * GLA_DK
GLA_VAL = GLA_HEADS * GLA_DV
GLA_RANK = 16
GLA_TAU = 16.0
GLA_SCALE = GLA_DK ** -0.5

DIFF_HEADS = 8
DIFF_HD = 64
DIFF_VD = 128
DIFF_QK = DIFF_HEADS * 2 * DIFF_HD
DIFF_VAL = DIFF_HEADS * DIFF_VD
ROT_DIM = DIFF_HD // 4
ROPE_THETA = 500000.0
QK_SCALE = DIFF_HD ** -0.5
Q_PRESCALE = QK_SCALE * math.log2(math.e)

IN_SIZES = (GLA_KEY, GLA_KEY, GLA_VAL, GLA_RANK, GLA_VAL,
            DIFF_QK, DIFF_QK, DIFF_VAL, DIFF_VAL, D_MODEL, D_MODEL)

LANES = 128
VMEM_LIMIT = 56 * 1024 * 1024

_NT = (((1,), (1,)), ((), ()))
_TN = (((0,), (0,)), ((), ()))


def _rms_rows(x, g):
    return x * lax.rsqrt(jnp.mean(x * x, axis=-1, keepdims=True) + EPS) * g


def _params(n_axes):
    return pltpu.CompilerParams(dimension_semantics=("arbitrary",) * n_axes,
                                vmem_limit_bytes=VMEM_LIMIT)


def _const_spec(shape):
    zeros = (0,) * len(shape)
    return pl.BlockSpec(shape, lambda *_: zeros, pipeline_mode=pl.Buffered(1))


def _qkv_body(x_ref, g_ref, w_ref, c_ref, s1_ref, s2_ref, k_ref, v_ref, q16_ref, k16_ref, v16_ref):
    xn = _rms_rows(x_ref[0], g_ref[...]).astype(BF16)
    c, s1, s2 = c_ref[...], s1_ref[...], s2_ref[...]

    def rope(t):
        return t * c + pltpu.roll(t, LANES - ROT_DIM // 2, 1) * s1 + pltpu.roll(t, ROT_DIM // 2, 1) * s2

    q = jnp.dot(xn, w_ref[:, 0:DIFF_QK], preferred_element_type=F32)
    for h in range(DIFF_HEADS):
        sl = slice(h * LANES, (h + 1) * LANES)
        q16_ref[0, h] = (rope(q[:, sl]) * Q_PRESCALE).astype(BF16)
    k = jnp.dot(xn, w_ref[:, DIFF_QK:2 * DIFF_QK], preferred_element_type=F32)
    for h in range(DIFF_HEADS):
        sl = slice(h * LANES, (h + 1) * LANES)
        kr = rope(k[:, sl])
        k_ref[0, :, sl] = kr
        k16_ref[0, h] = kr.astype(BF16)
    v = jnp.dot(xn, w_ref[:, 2 * DIFF_QK:], preferred_element_type=F32)
    v_ref[0] = v
    for h in range(DIFF_HEADS):
        v16_ref[0, h] = v[:, h * LANES:(h + 1) * LANES].astype(BF16)


def _qkv_call(x, g, w, tabs, tm):
    B, T, _ = x.shape
    row = pl.BlockSpec((1, tm, D_MODEL), lambda b, i: (b, i, 0))
    heads = pl.BlockSpec((1, DIFF_HEADS, tm, LANES), lambda b, i: (b, 0, i, 0))
    tab = pl.BlockSpec((tm, LANES), lambda b, i: (i, 0))
    head_major = jax.ShapeDtypeStruct((B, DIFF_HEADS, T, LANES), BF16)
    return pl.pallas_call(
        _qkv_body,
        grid=(B, T // tm),
        in_specs=[row, _const_spec((1, D_MODEL)), _const_spec(w.shape), tab, tab, tab],
        out_specs=[row, row, heads, heads, heads],
        out_shape=[jax.ShapeDtypeStruct((B, T, DIFF_QK), F32),
                   jax.ShapeDtypeStruct((B, T, DIFF_VAL), F32),
                   head_major, head_major, head_major],
        compiler_params=_params(2),
        name="qkv_proj",
    )(x, g, w, *tabs)


def _rope_tables(pos):
    half = ROT_DIM // 2
    lane = jnp.arange(ROT_DIM, dtype=jnp.int32)
    inv = ROPE_THETA ** (-(2 * (lane % half)).astype(F32) / ROT_DIM)
    ang = pos.astype(F32)[:, None] * inv[None, :]
    cos, sin = jnp.cos(ang), jnp.sin(ang)
    s1 = jnp.where(lane[None, :] < half, -sin, 0.0)
    s2 = jnp.where(lane[None, :] >= half, sin, 0.0)

    def widen(t, fill):
        t = jnp.pad(t, ((0, 0), (0, DIFF_HD - ROT_DIM)), constant_values=fill)
        return jnp.tile(t, (1, LANES // DIFF_HD))

    return widen(cos, 1.0), widen(s1, 0.0), widen(s2, 0.0)


def _gla_body(x_ref, g_ref, w_ref, wa_ref, ba_ref, s0_ref, o_ref, s_ref, st_ref, *, chunk):
    t = pl.program_id(1)
    C = chunk

    @pl.when(t == 0)
    def _():
        for h in range(GLA_HEADS):
            st_ref[h] = s0_ref[0, h].T

    ri = lax.broadcasted_iota(jnp.int32, (C, C), 0)
    ci = lax.broadcasted_iota(jnp.int32, (C, C), 1)
    tril = ri >= ci
    tril_b = jnp.where(tril, 1.0, 0.0).astype(BF16)

    heads = range(GLA_HEADS)
    ks = [slice(h * GLA_DK, (h + 1) * GLA_DK) for h in heads]
    vs = [slice(h * GLA_DV, (h + 1) * GLA_DV) for h in heads]

    n_chunks = x_ref.shape[1] // C
    xn = _rms_rows(x_ref[0], g_ref[...]).astype(BF16)
    gr = jnp.dot(xn, w_ref[:, 2 * GLA_KEY + GLA_VAL:], preferred_element_type=F32)
    z = jnp.dot(gr.astype(BF16), wa_ref[...], preferred_element_type=F32) + ba_ref[...]
    la_all = (jnp.minimum(z, 0.0) - jnp.log(1.0 + jnp.exp(-jnp.abs(z)))) * (1.0 / GLA_TAU)
    u_all = jnp.dot(xn, w_ref[:, 0:2 * GLA_KEY + GLA_VAL], preferred_element_type=F32)

    def operands(c):
        u, la = u_all[c * C:(c + 1) * C], la_all[c * C:(c + 1) * C]
        hi = la.astype(BF16)
        lo = (la - hi.astype(F32)).astype(BF16)
        b = (jnp.dot(tril_b, hi, preferred_element_type=F32)
             + jnp.dot(tril_b, lo, preferred_element_type=F32))
        r = b[C // 2 - 1:C // 2]
        bl = b[C - 1:C]
        q1 = u[:, 0:GLA_KEY] * GLA_SCALE * jnp.exp(b - r)
        qs = (q1 * jnp.exp(r)).astype(BF16)
        k1 = u[:, GLA_KEY:2 * GLA_KEY] * jnp.exp(r - b)
        kd = (k1 * jnp.exp(bl - r)).astype(BF16)
        v = u[:, 2 * GLA_KEY:2 * GLA_KEY + GLA_VAL].astype(BF16)
        return q1.astype(BF16), qs, k1.astype(BF16), kd, v, jnp.exp(bl)

    ops = [operands(c) for c in range(n_chunks)]
    a_raw = [[lax.dot_general(q1[:, ks[h]], k1[:, ks[h]], _NT, preferred_element_type=F32) for h in heads]
             for (q1, qs, k1, kd, v, dec) in ops]
    kv = [[lax.dot_general(v[:, vs[h]], kd[:, ks[h]], _TN, preferred_element_type=F32) for h in heads]
          for (q1, qs, k1, kd, v, dec) in ops]
    intra = [[jnp.dot(jnp.where(tril, a_raw[c][h], 0.0).astype(BF16), ops[c][4][:, vs[h]],
                      preferred_element_type=F32) for h in heads] for c in range(n_chunks)]
    for c in range(n_chunks):
        q1, qs, k1, kd, v, dec = ops[c]
        rows = slice(c * C, (c + 1) * C)
        for h in heads:
            st = st_ref[h]
            o_ref[0, rows, vs[h]] = intra[c][h] + lax.dot_general(qs[:, ks[h]], st.astype(BF16), _NT,
                                                                  preferred_element_type=F32)
            st_ref[h] = st * dec[:, ks[h]] + kv[c][h]

    @pl.when(t == pl.num_programs(1) - 1)
    def _():
        for h in range(GLA_HEADS):
            s_ref[0, h] = st_ref[h].T


def _gla_call(x, g, w, wa, ba, s0, layer, tt, chunk):
    B, T, _ = x.shape
    row = pl.BlockSpec((1, tt, D_MODEL), lambda b, i: (b, i, 0))
    state_shape = (1, GLA_HEADS, GLA_DK, GLA_DV)
    return pl.pallas_call(
        functools.partial(_gla_body, chunk=chunk),
        grid=(B, T // tt),
        in_specs=[row, _const_spec((1, D_MODEL)), _const_spec(w.shape), _const_spec(wa.shape),
                  _const_spec((1, GLA_KEY)), pl.BlockSpec(state_shape, lambda b, i: (layer * B + b, 0, 0, 0))],
        out_specs=[row, pl.BlockSpec(state_shape, lambda b, i: (b, 0, 0, 0))],
        out_shape=[jax.ShapeDtypeStruct((B, T, GLA_VAL), F32),
                   jax.ShapeDtypeStruct((B, GLA_HEADS, GLA_DK, GLA_DV), F32)],
        scratch_shapes=[pltpu.VMEM((GLA_HEADS, GLA_DV, GLA_DK), F32)],
        compiler_params=_params(2),
        name="gla",
    )(x, g, w, wa, ba, s0)


def _lam(lq1, lk1, lq2, lk2, lambda_init):
    return (jnp.exp(jnp.sum(lq1[...] * lk1[...], axis=-1, keepdims=True))
            - jnp.exp(jnp.sum(lq2[...] * lk2[...], axis=-1, keepdims=True)) + lambda_init)


def _split_maps(q):
    lane = lax.broadcasted_iota(jnp.int32, q.shape, 1)
    zero = jnp.zeros_like(q)
    return jnp.where(lane < DIFF_HD, q, zero), jnp.where(lane >= DIFF_HD, q, zero)


def _attn_body(lq1, lk1, lq2, lk2, q_ref, k_ref, v_ref, bias_ref, *rest, n, tq, lambda_init):
    o_ref, s_ref = rest[-2:]
    heads = q_ref.shape[1]
    nblk = tq // LANES
    hq = tq // 2

    def lane_fold(s):
        return functools.reduce(jnp.maximum, [s[:, c * LANES:(c + 1) * LANES] for c in range(s.shape[1] // LANES)])

    mx = {}
    for g in range(heads):
        qz = _split_maps(q_ref[0, g])
        for j in range(n):
            kt = k_ref[0, g, j * tq:(j + 1) * tq, :]
            for m in range(2):
                if j < n - 1:
                    s = lax.dot_general(qz[m], kt, _NT, preferred_element_type=F32)
                    s_ref[g, m, j] = s
                    fold = lane_fold(s)
                else:
                    top = (lax.dot_general(qz[m][:hq], kt[:hq], _NT, preferred_element_type=F32)
                           + bias_ref[:hq, :hq])
                    bot = lax.dot_general(qz[m][hq:], kt, _NT, preferred_element_type=F32) + bias_ref[hq:, :]
                    s_ref[g, m, j, :hq, :hq] = top
                    s_ref[g, m, j, hq:, :] = bot
                    fold = jnp.concatenate([lane_fold(top), lane_fold(bot)], axis=0)
                mx[g, m] = fold if j == 0 else jnp.maximum(mx[g, m], fold)
    ones = jnp.ones((tq, LANES), BF16)
    lam = _lam(lq1, lk1, lq2, lk2, lambda_init)
    for g in range(heads):
        acc = []
        for m in range(2):
            mrow = jnp.broadcast_to(jnp.max(mx[g, m], axis=-1, keepdims=True), (tq, LANES))
            mrow = jnp.concatenate([mrow] * nblk, axis=1)
            a = None
            for j in range(n):
                vt = jnp.concatenate([v_ref[0, g, j * tq:(j + 1) * tq, :], ones], axis=1)
                if j < n - 1:
                    p = jnp.exp2(s_ref[g, m, j] - mrow).astype(BF16)
                    d = jnp.dot(p, vt, preferred_element_type=F32)
                else:
                    p_top = jnp.exp2(s_ref[g, m, j, :hq, :hq] - mrow[:hq, :hq]).astype(BF16)
                    p_bot = jnp.exp2(s_ref[g, m, j, hq:, :] - mrow[hq:]).astype(BF16)
                    d = jnp.concatenate([jnp.dot(p_top, vt[:hq], preferred_element_type=F32),
                                         jnp.dot(p_bot, vt, preferred_element_type=F32)], axis=0)
                a = d if a is None else a + d
            acc.append(a)
        o = (acc[0][:, 0:DIFF_VD] / acc[0][:, DIFF_VD:]
             - lam * (acc[1][:, 0:DIFF_VD] / acc[1][:, DIFF_VD:])).astype(o_ref.dtype)
        cols = slice(g * DIFF_VD, (g + 1) * DIFF_VD)
        if o_ref.shape[1] == tq:
            o_ref[0, :, cols] = o
        else:
            o_ref[0, (n - 1) * tq:n * tq, cols] = o
            o_ref[0, 0:(n - 1) * tq, cols] = jnp.zeros(((n - 1) * tq, DIFF_VD), o_ref.dtype)


ATTN_SCORE_BYTES = 32 * 1024 * 1024


def _attn_call(lams, q, k, v, bias, o_prev, n, tq, lambda_init):
    B, _, T, _ = q.shape
    assert tq % (2 * LANES) == 0 and (tq // 2) % CHUNK == 0, tq
    heads = DIFF_HEADS
    while heads > 1 and heads * 2 * n * tq * tq * 4 > ATTN_SCORE_BYTES:
        heads //= 2
    lam_spec = _const_spec((1, DIFF_HD))
    qs = pl.BlockSpec((1, heads, tq, LANES), lambda b, h: (b, h, n - 1, 0))
    kv = pl.BlockSpec((1, heads, n * tq, LANES), lambda b, h: (b, h, 0, 0))
    in_specs = [lam_spec] * 4 + [qs, kv, kv, _const_spec((tq, tq))]
    args = (*lams, q, k, v, bias)
    if o_prev is None:
        assert n * tq == T, (n, tq, T)
        aliases = {}
        out_spec = pl.BlockSpec((1, T, heads * DIFF_VD), lambda b, h: (b, 0, h))
    else:
        in_specs.append(pl.BlockSpec(memory_space=pl.ANY))
        aliases = {len(args): 0}
        args = args + (o_prev,)
        out_spec = pl.BlockSpec((1, tq, heads * DIFF_VD), lambda b, h: (b, n - 1, h))
    return pl.pallas_call(
        functools.partial(_attn_body, n=n, tq=tq, lambda_init=lambda_init),
        grid=(B, DIFF_HEADS // heads),
        in_specs=in_specs,
        out_specs=out_spec,
        out_shape=jax.ShapeDtypeStruct((B, T, DIFF_VAL), BF16),
        scratch_shapes=[pltpu.VMEM((heads, 2, n, tq, tq), F32)],
        input_output_aliases=aliases,
        compiler_params=_params(2),
        name=f"diff_attn_q{n - 1}",
    )(*args)


def _diag_bias(tq):
    r = jnp.arange(tq, dtype=jnp.int32) // CHUNK
    return jnp.where(r[:, None] >= r[None, :], 0.0, -jnp.inf).astype(F32)


def _attn_cache_body(lq1, lk1, lq2, lk2, q_ref, kc_ref, vc_ref, kn_ref, vn_ref, o_ref, *, lambda_init):
    T = q_ref.shape[2]
    P = kc_ref.shape[1] // DIFF_HEADS
    lam = _lam(lq1, lk1, lq2, lk2, lambda_init)
    for h in range(DIFF_HEADS):
        qz = jnp.concatenate(_split_maps(q_ref[0, h]), axis=0)
        kc = kc_ref[0, pl.ds(h, P, stride=DIFF_HEADS), :].astype(BF16)
        vc = vc_ref[0, pl.ds(h, P, stride=DIFF_HEADS), :].astype(BF16)
        sc = lax.dot_general(qz, kc, _NT, preferred_element_type=F32)
        sn = lax.dot_general(qz, kn_ref[0, h], _NT, preferred_element_type=F32)
        mx = jnp.maximum(jnp.max(sc, axis=-1, keepdims=True), jnp.max(sn, axis=-1, keepdims=True))
        pc = jnp.exp2(sc - mx)
        pn = jnp.exp2(sn - mx)
        l = jnp.sum(pc, axis=-1, keepdims=True) + jnp.sum(pn, axis=-1, keepdims=True)
        o = (jnp.dot(pc.astype(BF16), vc, preferred_element_type=F32)
             + jnp.dot(pn.astype(BF16), vn_ref[0, h], preferred_element_type=F32)) / l
        o_ref[0, :, h * DIFF_VD:(h + 1) * DIFF_VD] = o[0:T] - lam * o[T:2 * T]


def _attn_cache_call(lams, q, kc, vc, kn, vn, layer, lambda_init):
    B, _, T, _ = q.shape
    lam_spec = _const_spec((1, DIFF_HD))
    new = pl.BlockSpec((1, DIFF_HEADS, T, LANES), lambda b: (b, 0, 0, 0))
    past = pl.BlockSpec((1, kc.shape[1], LANES), lambda b: (layer * B + b, 0, 0))
    return pl.pallas_call(
        functools.partial(_attn_cache_body, lambda_init=lambda_init),
        grid=(B,),
        in_specs=[lam_spec] * 4 + [new, past, past, new, new],
        out_specs=pl.BlockSpec((1, T, DIFF_VAL), lambda b: (b, 0, 0)),
        out_shape=jax.ShapeDtypeStruct((B, T, DIFF_VAL), F32),
        compiler_params=_params(1),
        name="diff_attn_cache",
    )(*lams, q, kc, vc, kn, vn)


MERGE_ROWS = 256


def _merge_body(x_ref, og_ref, od_ref, gpre_ref, wg_ref, ggla_ref, gsub_ref, wpg_ref, wpd_ref, wo_ref,
                gpost_ref, y_ref, yg_s, yd_s, *, lambda_init):
    tm = x_ref.shape[0]
    blocks = [slice(r, r + min(MERGE_ROWS, tm)) for r in range(0, tm, min(MERGE_ROWS, tm))]
    gates = [jnp.dot(_rms_rows(x_ref[rows], gpre_ref[...]).astype(BF16), wg_ref[...],
                     preferred_element_type=F32) for rows in blocks]
    for rows, gt in zip(blocks, gates):
        og, od = og_ref[rows], od_ref[rows].astype(F32)
        for h in range(GLA_HEADS):
            sl = slice(h * GLA_DV, (h + 1) * GLA_DV)
            yg_s[rows, sl] = (_rms_rows(og[:, sl], ggla_ref[...]) * jax.nn.silu(gt[:, sl])).astype(BF16)
        for h in range(DIFF_HEADS):
            sl = slice(h * DIFF_VD, (h + 1) * DIFF_VD)
            dg = gt[:, GLA_VAL + h * DIFF_VD:GLA_VAL + (h + 1) * DIFF_VD]
            yd_s[rows, sl] = (_rms_rows(od[:, sl], gsub_ref[...]) * (1.0 - lambda_init)
                              * jax.nn.silu(dg)).astype(BF16)
    merged = []
    for rows, gt in zip(blocks, gates):
        ma = gt[:, GLA_VAL + DIFF_VAL:GLA_VAL + DIFF_VAL + D_MODEL]
        mb = gt[:, GLA_VAL + DIFF_VAL + D_MODEL:]
        merged.append(jax.nn.sigmoid(ma) * jnp.dot(yg_s[rows], wpg_ref[...], preferred_element_type=F32)
                      + jax.nn.sigmoid(mb) * jnp.dot(yd_s[rows], wpd_ref[...], preferred_element_type=F32))
    for rows, mg in zip(blocks, merged):
        z = jnp.dot(mg.astype(BF16), wo_ref[...], preferred_element_type=F32)
        y_ref[rows] = x_ref[rows] + _rms_rows(z, gpost_ref[...])


def _merge_call(x, og, od, gpre, wg, ggla, gsub, wpg, wpd, wo, gpost, tm, lambda_init):
    N = x.shape[0]
    row = pl.BlockSpec((tm, D_MODEL), lambda i: (i, 0))
    return pl.pallas_call(
        functools.partial(_merge_body, lambda_init=lambda_init),
        grid=(N // tm,),
        in_specs=[row, row, row, _const_spec((1, D_MODEL)), _const_spec(wg.shape), _const_spec((1, GLA_DV)),
                  _const_spec((1, DIFF_VD)), _const_spec(wpg.shape), _const_spec(wpd.shape),
                  _const_spec(wo.shape), _const_spec((1, D_MODEL))],
        out_specs=row,
        out_shape=jax.ShapeDtypeStruct((N, D_MODEL), F32),
        scratch_shapes=[pltpu.VMEM((tm, GLA_VAL), BF16), pltpu.VMEM((tm, DIFF_VAL), BF16)],
        compiler_params=_params(1),
        name="merge",
    )(x, og, od, gpre, wg, ggla, gsub, wpg, wpd, wo, gpost)


def _tile(n, target):
    t = min(n, target)
    while n % t:
        t -= 1
    return t


def kernel(x_prompt, x_sample, cache_diff_k, cache_diff_v, state_gla, pre_norm_g, w_in, gla_w_a2, gla_b_a,
           gla_norm_g, diff_lambda_q1, diff_lambda_k1, diff_lambda_q2, diff_lambda_k2, diff_subln_g,
           w_proj_gla, w_proj_diff, w_out, post_norm_g):
    depth = w_in.shape[0]
    past_len = cache_diff_k.shape[2]
    hp, hs = x_prompt, x_sample
    B, T, _ = hp.shape
    Bs, Ts, _ = hs.shape
    cache_k = cache_diff_k.reshape(depth * Bs, past_len * DIFF_HEADS, 2 * DIFF_HD)
    cache_v = cache_diff_v.reshape(depth * Bs, past_len * DIFF_HEADS, DIFF_VD)
    state_in = state_gla.astype(F32).reshape(depth * Bs, GLA_HEADS, GLA_DK, GLA_DV)
    zero_state = jnp.zeros((B, GLA_HEADS, GLA_DK, GLA_DV), F32)
    outs = [[] for _ in range(6)]
    offs = [0]
    for s in IN_SIZES:
        offs.append(offs[-1] + s)
    for l in range(depth):
        lambda_init = 0.8 - 0.6 * math.exp(-0.3 * l)
        def cols(first, last):
            return w_in[l, :, offs[first]:offs[last + 1]].astype(BF16)

        w_gla = jnp.concatenate([cols(0, 2), jnp.pad(cols(3, 3), ((0, 0), (0, LANES - GLA_RANK)))], axis=1)
        w_qkv = cols(5, 7)
        w_gate = jnp.concatenate([cols(4, 4), cols(8, 10)], axis=1)
        wa = jnp.pad(gla_w_a2[l], ((0, LANES - GLA_RANK), (0, 0))).astype(BF16)
        ba = gla_b_a[l][None]
        gpre = pre_norm_g[l][None]
        lams = (diff_lambda_q1[l][None], diff_lambda_k1[l][None], diff_lambda_q2[l][None], diff_lambda_k2[l][None])
        merge_w = (gpre, w_gate, gla_norm_g[l][None], diff_subln_g[l][None], w_proj_gla[l].astype(BF16),
                   w_proj_diff[l].astype(BF16), w_out[l].astype(BF16), post_norm_g[l][None])

        k, v, q16, k16, v16 = _qkv_call(hp, gpre, w_qkv, _rope_tables(jnp.arange(T, dtype=jnp.int32)),
                                        _tile(T, 512))
        o_gla, s_p = _gla_call(hp, gpre, w_gla, wa, ba, zero_state, 0, _tile(T, 1024), _tile(T, 128))
        tq = _tile(T, 512)
        bias = _diag_bias(tq)
        o_diff = None
        for n in range(T // tq, 0, -1):
            o_diff = _attn_call(lams, q16, k16, v16, bias, o_diff, n, tq, lambda_init)
        hp = _merge_call(hp.reshape(B * T, D_MODEL), o_gla.reshape(B * T, GLA_VAL),
                         o_diff.reshape(B * T, DIFF_VAL), *merge_w, _tile(B * T, 512),
                         lambda_init).reshape(B, T, D_MODEL)
        outs[0].append(k.reshape(B, T, DIFF_HEADS, DIFF_VD))
        outs[1].append(v.reshape(B, T, DIFF_HEADS, DIFF_VD))
        outs[2].append(s_p)

        pos_s = past_len + jnp.arange(Ts, dtype=jnp.int32)
        k, v, q16, k16, v16 = _qkv_call(hs, gpre, w_qkv, _rope_tables(pos_s), Ts)
        o_gla, s_s = _gla_call(hs, gpre, w_gla, wa, ba, state_in, l, Ts, Ts)
        o_diff = _attn_cache_call(lams, q16, cache_k, cache_v, k16, v16, l, lambda_init)
        hs = _merge_call(hs.reshape(Bs * Ts, D_MODEL), o_gla.reshape(Bs * Ts, GLA_VAL),
                         o_diff.reshape(Bs * Ts, DIFF_VAL), *merge_w, _tile(Bs * Ts, 256),
                         lambda_init).reshape(Bs, Ts, D_MODEL)
        outs[3].append(k.reshape(Bs, Ts, DIFF_HEADS, DIFF_VD))
        outs[4].append(v.reshape(Bs, Ts, DIFF_HEADS, DIFF_VD))
        outs[5].append(s_s)

    return (hp, hs, jnp.stack(outs[0]), jnp.stack(outs[1]), jnp.stack(outs[2]),
            jnp.stack(outs[3]), jnp.stack(outs[4]), jnp.stack(outs[5]))
```
